```python
import jax, jax.numpy as jnp
from jax import lax
import numpy as np

D_MODEL = 1024
BATCH = 8
SEQ = 2048
DEPTH = 2

N_RET_HEADS = 4
RET_QK_DIM = 256
RET_V_DIM = 512
RET_CHUNK = 128
N_ATT_HEADS = 8
ATT_HEAD_DIM = 128
N_IDX_HEADS = 8
IDX_DIM = 64
TOPK_MAX = 256
Q_BLOCK = 128
N_GROUPS = 4
EXPERTS_PER_GROUP = 8
N_EXPERTS = N_GROUPS * EXPERTS_PER_GROUP
TOP_K_EXPERTS = 2
EXPERT_FF = 512
DISPATCH_BLOCK = 128
ROPE_THETA = 10000.0
LN_EPS = 1e-5
GN_EPS = 1e-5
DEEPNORM_ALPHA = (2 * DEPTH) ** 0.25
DEEPNORM_BETA = (8 * DEPTH) ** -0.25

RET_Q_W = N_RET_HEADS * RET_QK_DIM
RET_V_W = N_RET_HEADS * RET_V_DIM
ATT_Q_W = N_ATT_HEADS * ATT_HEAD_DIM
IDX_Q_W = N_IDX_HEADS * IDX_DIM
COL_SIZES = (RET_Q_W, RET_Q_W, RET_V_W, RET_V_W, ATT_Q_W, ATT_HEAD_DIM, ATT_HEAD_DIM,
             IDX_Q_W, IDX_DIM, N_IDX_HEADS, D_MODEL, D_MODEL)
IN_WIDTH = sum(COL_SIZES)

kernel_name = 'hybrid_retention_dsa_hiermoe_deepnorm'


def layer_norm(x, g, b):
    xf = x.astype(jnp.float32)
    mu = xf.mean(-1, keepdims=True)
    var = jnp.square(xf - mu).mean(-1, keepdims=True)
    y = (xf - mu) * lax.rsqrt(var + LN_EPS)
    return (y * g.astype(jnp.float32) + b.astype(jnp.float32)).astype(x.dtype)


def rotary(x, pos):
    d = x.shape[-1]
    inv = 1.0 / (ROPE_THETA ** (jnp.arange(0, d, 2, dtype=jnp.float32) / d))
    ang = pos.astype(jnp.float32)[:, None] * inv[None, :]
    cos = jnp.cos(ang)[None, :, None, :].astype(x.dtype)
    sin = jnp.sin(ang)[None, :, None, :].astype(x.dtype)
    x1, x2 = x[..., : d // 2], x[..., d // 2:]
    return jnp.concatenate([x1 * cos - x2 * sin, x2 * cos + x1 * sin], axis=-1)


def retention(q, k, v, g):
    B, S, H, dk = q.shape
    dv = v.shape[-1]
    C = RET_CHUNK
    N = S // C
    log_gamma = jnp.log(1.0 - 2.0 ** (-5.0 - jnp.arange(H, dtype=jnp.float32)))
    pos = jnp.arange(C, dtype=jnp.float32)
    diff = pos[:, None] - pos[None, :]
    decay_intra = jnp.where(diff >= 0,
                            jnp.exp(jnp.maximum(diff, 0.0)[None] * log_gamma[:, None, None]),
                            0.0)
    xi = jnp.exp((pos[:, None] + 1.0) * log_gamma[None, :])
    zeta = jnp.exp((C - 1.0 - pos)[:, None] * log_gamma[None, :])
    gamma_c = jnp.exp(C * log_gamma)
    k = k * (dk ** -0.5)
    qc = q.reshape(B, N, C, H, dk)
    kc = k.reshape(B, N, C, H, dk)
    vc = v.reshape(B, N, C, H, dv)
    scores = jnp.einsum('bnchd,bnmhd->bnhcm', qc, kc) * decay_intra.astype(q.dtype)
    intra = jnp.einsum('bnhcm,bnmhe->bnche', scores, vc)

    def step(R, xs):
        qn, kn, vn = xs
        cross = jnp.einsum('bchd,bhde->bche', qn * xi[None, :, :, None].astype(qn.dtype),
                           R.astype(qn.dtype))
        upd = jnp.einsum('bchd,bche->bhde', kn * zeta[None, :, :, None].astype(kn.dtype), vn)
        R = gamma_c[None, :, None, None] * R + upd.astype(jnp.float32)
        return R, cross

    R0 = jnp.zeros((B, H, dk, dv), jnp.float32)
    _, cross = lax.scan(step, R0, (jnp.moveaxis(qc, 1, 0), jnp.moveaxis(kc, 1, 0),
                                   jnp.moveaxis(vc, 1, 0)))
    y = (intra + jnp.moveaxis(cross, 0, 1)).reshape(B, S, H, dv).astype(jnp.float32)
    mu = y.mean(-1, keepdims=True)
    var = jnp.square(y - mu).mean(-1, keepdims=True)
    yn = ((y - mu) * lax.rsqrt(var + GN_EPS)).reshape(B, S, H * dv).astype(g.dtype)
    return yn * jax.nn.silu(g)


def dsa_attention(q, k, v, qi, ki, wi):
    B, S, H, hd = q.shape
    K = min(TOPK_MAX, S // 4)
    NB = S // Q_BLOCK
    key_pos = jnp.arange(S)
    idx_scale = (IDX_DIM ** -0.5) * (N_IDX_HEADS ** -0.5)
    gather = jax.vmap(lambda a, i: a[i])

    def block(xs):
        qb, qib, wib, t0 = xs
        t = t0 + jnp.arange(Q_BLOCK)
        rel = jax.nn.relu(jnp.einsum('bthd,bsd->bths', qib, ki).astype(jnp.float32))
        score = jnp.einsum('bths,bth->bts', rel, wib.astype(jnp.float32)) * idx_scale
        visible = key_pos[None, None, :] <= t[None, :, None]
        score = jnp.where(visible, score, -jnp.inf)
        _, sel = lax.top_k(score, K)
        k_sel = gather(k, sel)
        v_sel = gather(v, sel)
        s = jnp.einsum('bthd,btkd->bthk', qb, k_sel).astype(jnp.float32) * (hd ** -0.5)
        valid = (sel <= t[None, :, None])[:, :, None, :]
        p = jax.nn.softmax(jnp.where(valid, s, -jnp.inf), axis=-1)
        return jnp.einsum('bthk,btkd->bthd', p.astype(v.dtype), v_sel)

    def to_blocks(a):
        return jnp.moveaxis(a.reshape((B, NB, Q_BLOCK) + a.shape[2:]), 1, 0)

    out = lax.map(block, (to_blocks(q), to_blocks(qi), to_blocks(wi),
                          jnp.arange(NB) * Q_BLOCK))
    return jnp.moveaxis(out, 0, 1).reshape(B, S, H * hd)


def hybrid_mixer(x, w_in, w_ret_o, w_dsa_o, w_out):
    B, S, _ = x.shape
    pos = jnp.arange(S)
    proj = x @ w_in
    offsets = [int(o) for o in np.cumsum(COL_SIZES)[:-1]]
    rq, rk, rv, rg, aq, ak, av, iq, ik, iw, gr, ga = jnp.split(proj, offsets, axis=-1)
    rq = rotary(rq.reshape(B, S, N_RET_HEADS, RET_QK_DIM), pos)
    rk = rotary(rk.reshape(B, S, N_RET_HEADS, RET_QK_DIM), pos)
    rv = rv.reshape(B, S, N_RET_HEADS, RET_V_DIM)
    aq = rotary(aq.reshape(B, S, N_ATT_HEADS, ATT_HEAD_DIM), pos)
    ak = rotary(ak[:, :, None, :], pos)[:, :, 0]
    iq = rotary(iq.reshape(B, S, N_IDX_HEADS, IDX_DIM), pos)
    ik = rotary(ik[:, :, None, :], pos)[:, :, 0]
    ret = retention(rq, rk, rv, rg)
    att = dsa_attention(aq, ak, av, iq, ik, iw)
    merged = jax.nn.sigmoid(gr) * (ret @ w_ret_o) + jax.nn.sigmoid(ga) * (att @ w_dsa_o)
    return merged @ w_out


def hierarchical_moe(h, grp_w, grp_b, exp_w, exp_b, w_gate, w_up, w_down):
    B, S, D = h.shape
    T = B * S
    xt = h.reshape(T, D)
    grp_logits = (xt @ grp_w + grp_b).astype(jnp.float32)
    grp_prob = jax.nn.softmax(grp_logits, axis=-1)
    _, g_sel = lax.top_k(grp_logits, 1)
    p_grp = jnp.take_along_axis(grp_prob, g_sel, axis=1)
    exp_logits = (xt @ exp_w + exp_b).astype(jnp.float32).reshape(T, N_GROUPS, EXPERTS_PER_GROUP)
    in_grp = jnp.take_along_axis(exp_logits, g_sel[:, :, None], axis=1)[:, 0]
    top_logit, top_idx = lax.top_k(in_grp, TOP_K_EXPERTS)
    gate = p_grp * jax.nn.softmax(top_logit, axis=-1)
    expert = g_sel * EXPERTS_PER_GROUP + top_idx
    A = T * TOP_K_EXPERTS
    e_flat = expert.reshape(A)
    tok = jnp.repeat(jnp.arange(T), TOP_K_EXPERTS)
    w_flat = gate.reshape(A)
    order = jnp.argsort(e_flat)
    e_s, tok_s, w_s = e_flat[order], tok[order], w_flat[order]
    counts = jnp.zeros((N_EXPERTS,), jnp.int32).at[e_flat].add(1)
    starts = jnp.cumsum(counts) - counts
    padded = (counts + DISPATCH_BLOCK - 1) // DISPATCH_BLOCK * DISPATCH_BLOCK
    pad_ends = jnp.cumsum(padded)
    pad_starts = pad_ends - padded
    dest = pad_starts[e_s] + jnp.arange(A) - starts[e_s]
    n_blocks = -(-A // DISPATCH_BLOCK) + N_EXPERTS
    x_pad = jnp.zeros((n_blocks * DISPATCH_BLOCK, D), h.dtype).at[dest].set(xt[tok_s])
    block_expert = jnp.minimum(
        jnp.searchsorted(pad_ends, jnp.arange(n_blocks) * DISPATCH_BLOCK, side='right'),
        N_EXPERTS - 1)

    def expert_block(args):
        xb, e = args
        return (jax.nn.silu(xb @ w_gate[e]) * (xb @ w_up[e])) @ w_down[e]

    y_pad = lax.map(expert_block, (x_pad.reshape(n_blocks, DISPATCH_BLOCK, D), block_expert))
    y = y_pad.reshape(-1, D)[dest] * w_s[:, None].astype(h.dtype)
    return jax.ops.segment_sum(y, tok_s, num_segments=T).reshape(B, S, D)


def setup_inputs(seed: int = 0) -> dict:
    key = jax.random.key(seed)
    ks = jax.random.split(key, 17)

    def normal(k, shape, scale):
        return jax.random.normal(k, shape, jnp.float32) * scale

    ef = (DEPTH, N_EXPERTS, D_MODEL, EXPERT_FF)
    return {
        'x': normal(ks[0], (BATCH, SEQ, D_MODEL), 1.0),
        'w_in': normal(ks[1], (DEPTH, D_MODEL, IN_WIDTH), D_MODEL ** -0.5),
        'w_ret_o': normal(ks[2], (DEPTH, RET_V_W, D_MODEL), RET_V_W ** -0.5),
        'w_dsa_o': normal(ks[3], (DEPTH, ATT_Q_W, D_MODEL), ATT_Q_W ** -0.5),
        'w_out': normal(ks[4], (DEPTH, D_MODEL, D_MODEL), D_MODEL ** -0.5 * DEEPNORM_BETA),
        'ln1_g': 1.0 + normal(ks[5], (DEPTH, D_MODEL), 0.02),
        'ln1_b': normal(ks[6], (DEPTH, D_MODEL), 0.02),
        'router_grp_w': normal(ks[7], (DEPTH, D_MODEL, N_GROUPS), D_MODEL ** -0.5),
        'router_grp_b': normal(ks[8], (DEPTH, N_GROUPS), 0.01),
        'router_exp_w': normal(ks[9], (DEPTH, D_MODEL, N_EXPERTS), D_MODEL ** -0.5),
        'router_exp_b': normal(ks[10], (DEPTH, N_EXPERTS), 0.01),
        'w_gate': normal(ks[11], ef, D_MODEL ** -0.5),
        'w_up': normal(ks[12], ef, D_MODEL ** -0.5),
        'w_down': normal(ks[13], (DEPTH, N_EXPERTS, EXPERT_FF, D_MODEL),
                         EXPERT_FF ** -0.5 * DEEPNORM_BETA),
        'ln2_g': 1.0 + normal(ks[14], (DEPTH, D_MODEL), 0.02),
        'ln2_b': normal(ks[15], (DEPTH, D_MODEL), 0.02),
    }


def reference(x, w_in, w_ret_o, w_dsa_o, w_out, ln1_g, ln1_b, router_grp_w, router_grp_b,
              router_exp_w, router_exp_b, w_gate, w_up, w_down, ln2_g, ln2_b):
    for l in range(DEPTH):
        mix = hybrid_mixer(x, w_in[l], w_ret_o[l], w_dsa_o[l], w_out[l])
        h = layer_norm(DEEPNORM_ALPHA * x + mix, ln1_g[l], ln1_b[l])
        ffn = hierarchical_moe(h, router_grp_w[l], router_grp_b[l], router_exp_w[l],
                               router_exp_b[l], w_gate[l], w_up[l], w_down[l])
        x = layer_norm(DEEPNORM_ALPHA * h + ffn, ln2_g[l], ln2_b[l])
    return x
```

```python
import functools

import numpy as np
import jax
import jax.numpy as jnp
from jax import lax
from jax.experimental import pallas as pl
from jax.experimental.pallas import tpu as pltpu

D_MODEL_ = 1024
DEPTH_ = 2
RET_HEADS = 4
RET_DK = 256
RET_DV = 512
RET_C = 128
ATT_HEADS = 8
ATT_HD = 128
IDX_HEADS = 8
IDX_D = 64
TOPK = 256
QBLK = 128
N_GRP = 4
EXP_PER_GRP = 8
N_EXP = N_GRP * EXP_PER_GRP
EXP_FF = 512
DISP_BLK = 128
THETA = 10000.0
LN_EPS_ = 1e-5
GN_EPS_ = 1e-5
ALPHA = (2 * DEPTH_) ** 0.25

LANES = 128
INT_MIN = -(2 ** 31)

_MXU_DTYPE = jnp.bfloat16
_VMEM_LIMIT = 56 * 1024 * 1024


def _cparams(sem):
    return pltpu.CompilerParams(dimension_semantics=sem, vmem_limit_bytes=_VMEM_LIMIT)


def _dot(a, b):
    return jnp.dot(a, b, preferred_element_type=jnp.float32)


def _dot_nt(a, b):
    return lax.dot_general(a, b, (((1,), (1,)), ((), ())), preferred_element_type=jnp.float32)


def _dot_tn(a, b):
    return lax.dot_general(a, b, (((0,), (0,)), ((), ())), preferred_element_type=jnp.float32)


def _swap_halves(a, d):
    tn = a.shape[1]
    if d == 2 * LANES:
        parts = []
        for c in range(0, tn, d):
            parts += [a[:, c + LANES:c + d], a[:, c:c + LANES]]
        return jnp.concatenate(parts, axis=1)
    assert d == LANES
    return jnp.concatenate(
        [pltpu.roll(a[:, c:c + LANES], LANES // 2, axis=1) for c in range(0, tn, LANES)], axis=1)


def _proj_kernel(x_ref, w_ref, *rest, rot):
    if rot:
        c_ref, s_ref, o_ref = rest
    else:
        (o_ref,) = rest
    acc = _dot(x_ref[...], w_ref[...])
    if rot:
        acc = acc * c_ref[...] + _swap_halves(acc, rot) * s_ref[...]
    o_ref[...] = acc.astype(o_ref.dtype)


def _proj(x, w, seq, *, rot=0, tables=None, out_dtype=None, tm=1024, tn=512):
    T, K = x.shape
    N = w.shape[1]
    tn = min(tn, N)
    tm = min(tm, seq)
    n_seq = seq // tm
    in_specs = [pl.BlockSpec((tm, K), lambda i, j: (i, 0)),
                pl.BlockSpec((K, tn), lambda i, j: (0, j))]
    args = [x, w]
    if rot:
        in_specs += [pl.BlockSpec((tm, tn), lambda i, j: (i % n_seq, 0))] * 2
        args += list(tables)
    return pl.pallas_call(
        functools.partial(_proj_kernel, rot=rot),
        grid=(T // tm, N // tn),
        in_specs=in_specs,
        out_specs=pl.BlockSpec((tm, tn), lambda i, j: (i, j)),
        out_shape=jax.ShapeDtypeStruct((T, N), out_dtype or x.dtype),
        compiler_params=_cparams(("parallel", "parallel")),
        name="proj_rot%d" % rot,
    )(*args)


def _rot_tables(seq, d, width, layout):
    inv = 1.0 / (THETA ** (np.arange(0, d, 2, dtype=np.float64) / d))
    ang = np.arange(seq, dtype=np.float64)[:, None] * inv[None, :]
    cos, sin = np.cos(ang), np.sin(ang)
    if layout == "half":
        c = np.concatenate([cos, cos], axis=1)
        s = np.concatenate([-sin, sin], axis=1)
    else:
        z = np.zeros_like(cos)
        c = np.concatenate([cos, z, cos, z], axis=1)
        s = np.concatenate([-sin, z, sin, z], axis=1)
    reps = width // c.shape[1]
    return np.tile(c, (1, reps)).astype(np.float32), np.tile(s, (1, reps)).astype(np.float32)


def _ret_kernel(qk_ref, v_ref, g_ref, dec_ref, xi_ref, zeta_ref, o_ref, r_ref, *, gamma_c):
    H, DK, DV = RET_HEADS, RET_DK, RET_DV

    @pl.when(pl.program_id(1) == 0)
    def _():
        r_ref[...] = jnp.zeros_like(r_ref)

    for h in range(H):
        q = qk_ref[:, h * DK:(h + 1) * DK]
        k = qk_ref[:, (H + h) * DK:(H + h + 1) * DK]
        v = v_ref[:, h * DV:(h + 1) * DV]
        s = _dot_nt(q, k) * dec_ref[h]
        intra = _dot(s.astype(_MXU_DTYPE), v)
        r_old = r_ref[h]
        cross = _dot(q, r_old.astype(_MXU_DTYPE))
        xi = jnp.concatenate([xi_ref[h]] * (DV // LANES), axis=1)
        zeta = jnp.concatenate([zeta_ref[h]] * (DV // LANES), axis=1)
        y = intra + cross * xi
        vz = (v.astype(jnp.float32) * zeta).astype(_MXU_DTYPE)
        r_ref[h] = gamma_c[h] * r_old + _dot_tn(k, vz)
        mu = jnp.mean(y, axis=-1, keepdims=True)
        yc = y - mu
        var = jnp.mean(yc * yc, axis=-1, keepdims=True)
        yn = yc * lax.rsqrt(var + GN_EPS_)
        g = g_ref[:, h * DV:(h + 1) * DV].astype(jnp.float32)
        o_ref[:, h * DV:(h + 1) * DV] = (yn * (g * jax.nn.sigmoid(g))).astype(o_ref.dtype)


def _retention(qk, pv, batch, seq):
    H, DK, DV, C = RET_HEADS, RET_DK, RET_DV, RET_C
    n_chunks = seq // C
    lg = np.log(1.0 - 2.0 ** (-5.0 - np.arange(H, dtype=np.float64)))
    pos = np.arange(C, dtype=np.float64)
    diff = pos[:, None] - pos[None, :]
    dec = np.where(diff >= 0, np.exp(np.maximum(diff, 0.0)[None] * lg[:, None, None]), 0.0)
    dec = (dec * DK ** -0.5).astype(np.float32)
    ones = np.ones((1, 1, LANES))
    xi = (np.exp((pos[None, :] + 1.0) * lg[:, None])[:, :, None] * ones).astype(np.float32)
    zeta = (np.exp((C - 1.0 - pos)[None, :] * lg[:, None])[:, :, None] * ones
            * DK ** -0.5).astype(np.float32)
    gamma_c = tuple(float(g) for g in np.exp(C * lg))
    T = qk.shape[0]
    const3 = lambda b, n: (0, 0, 0)
    row = lambda b, n: (b * n_chunks + n, 0)
    return pl.pallas_call(
        functools.partial(_ret_kernel, gamma_c=gamma_c),
        grid=(batch, n_chunks),
        in_specs=[pl.BlockSpec((C, 2 * H * DK), row),
                  pl.BlockSpec((C, H * DV), row),
                  pl.BlockSpec((C, H * DV), lambda b, n: (b * n_chunks + n, 1)),
                  pl.BlockSpec((H, C, C), const3),
                  pl.BlockSpec((H, C, LANES), const3),
                  pl.BlockSpec((H, C, LANES), const3)],
        out_specs=pl.BlockSpec((C, H * DV), row),
        out_shape=jax.ShapeDtypeStruct((T, H * DV), qk.dtype),
        scratch_shapes=[pltpu.VMEM((H, DK, DV), jnp.float32)],
        compiler_params=_cparams(("parallel", "arbitrary")),
        name="retention",
    )(qk, pv, pv, jnp.asarray(dec), jnp.asarray(xi), jnp.asarray(zeta))


def _dsa_kernel(aq_ref, iq_ref, eq_ref, ek_ref, o_ref, k_s, ik_s, vt_s, key_s, bias_s, *, topk):
    S = ek_ref.shape[0]
    QB = aq_ref.shape[0]
    j = pl.program_id(1)

    @pl.when(j == 0)
    def _():
        k_s[...] = ek_ref[:, 0:LANES].astype(k_s.dtype)
        vt_s[...] = ek_ref[:, LANES:2 * LANES].T.astype(vt_s.dtype)
        ik_s[...] = ek_ref[:, 2 * LANES:3 * LANES].astype(ik_s.dtype)

    w_t = eq_ref[:, 3 * LANES:4 * LANES].T
    score = jnp.zeros((S, QB), jnp.float32)
    for h in range(IDX_HEADS):
        rel = jnp.maximum(_dot_nt(ik_s[...], iq_ref[:, h * LANES:(h + 1) * LANES]), 0.0)
        score = score + rel * w_t[h:h + 1, :]
    score = score * ((IDX_D ** -0.5) * (IDX_HEADS ** -0.5))
    key_pos = lax.broadcasted_iota(jnp.int32, (S, QB), 0)
    q_pos = j * QB + lax.broadcasted_iota(jnp.int32, (S, QB), 1)
    visible = key_pos <= q_pos
    bits = pltpu.bitcast(score, jnp.int32)
    bits = jnp.where(bits == INT_MIN, 0, bits)
    key = jnp.where(bits < 0, bits ^ 0x7FFFFFFF, bits)
    key_s[...] = jnp.where(visible, key, INT_MIN)

    def count_ge(c):
        return jnp.sum((key_s[...] >= c).astype(jnp.int32), axis=0, keepdims=True)

    prefix = jnp.where(count_ge(jnp.zeros((1, QB), jnp.int32)) >= topk, 0, INT_MIN)

    def bit_step(i, prefix):
        cand = prefix | lax.shift_left(jnp.int32(1), 30 - i)
        return jnp.where(count_ge(cand) >= topk, cand, prefix)

    thr = lax.fori_loop(0, 31, bit_step, prefix)
    need = topk - jnp.sum((key_s[...] > thr).astype(jnp.int32), axis=0, keepdims=True)
    need_f = need.astype(jnp.float32)
    r_i = lax.broadcasted_iota(jnp.int32, (LANES, LANES), 0)
    c_i = lax.broadcasted_iota(jnp.int32, (LANES, LANES), 1)
    lower = (r_i > c_i).astype(_MXU_DTYPE)
    carry = jnp.zeros((1, QB), jnp.float32)
    for c in range(S // LANES):
        sl = slice(c * LANES, (c + 1) * LANES)
        key_c = key_s[sl, :]
        eq_c = key_c == thr
        eq_f = eq_c.astype(jnp.float32)
        before = _dot(lower, eq_f.astype(_MXU_DTYPE)) + carry
        take = ((key_c > thr) | (eq_c & (before < need_f))) & (key_c != INT_MIN)
        bias_s[sl, :] = jnp.where(take, 0.0, -jnp.inf)
        carry = carry + jnp.sum(eq_f, axis=0, keepdims=True)

    for h in range(ATT_HEADS):
        s = _dot_nt(k_s[...], aq_ref[:, h * ATT_HD:(h + 1) * ATT_HD]) * (ATT_HD ** -0.5)
        s = s + bias_s[...]
        m = jnp.max(s, axis=0, keepdims=True)
        p = jnp.exp(s - m)
        l = jnp.sum(p, axis=0, keepdims=True)
        o_t = _dot(vt_s[...], p.astype(_MXU_DTYPE)) / l
        o_ref[:, h * ATT_HD:(h + 1) * ATT_HD] = o_t.T.astype(o_ref.dtype)


def _dsa(aq, iq, e, batch, seq):
    T = aq.shape[0]
    nb = seq // QBLK
    topk = min(TOPK, seq // 4)
    qrow = lambda b, j: (b * nb + j, 0)
    return pl.pallas_call(
        functools.partial(_dsa_kernel, topk=topk),
        grid=(batch, nb),
        in_specs=[pl.BlockSpec((QBLK, ATT_HEADS * ATT_HD), qrow),
                  pl.BlockSpec((QBLK, IDX_HEADS * LANES), qrow),
                  pl.BlockSpec((QBLK, 4 * LANES), qrow),
                  pl.BlockSpec((seq, 4 * LANES), lambda b, j: (b, 0))],
        out_specs=pl.BlockSpec((QBLK, ATT_HEADS * ATT_HD), qrow),
        out_shape=jax.ShapeDtypeStruct((T, ATT_HEADS * ATT_HD), aq.dtype),
        scratch_shapes=[pltpu.VMEM((seq, LANES), aq.dtype),
                        pltpu.VMEM((seq, LANES), aq.dtype),
                        pltpu.VMEM((LANES, seq), aq.dtype),
                        pltpu.VMEM((seq, QBLK), jnp.int32),
                        pltpu.VMEM((seq, QBLK), jnp.float32)],
        compiler_params=_cparams(("parallel", "arbitrary")),
        name="dsa",
    )(aq, iq, e, e)


def _layer_norm(z, g, b):
    mu = jnp.mean(z, axis=-1, keepdims=True)
    zc = z - mu
    var = jnp.mean(zc * zc, axis=-1, keepdims=True)
    return zc * lax.rsqrt(var + LN_EPS_) * g + b


def _merge_kernel(ret_ref, att_ref, gr_ref, ga_ref, x_ref, wr_ref, wd_ref, wo_ref, g_ref, b_ref,
                  rw_ref, rb_ref, h_ref, lg_ref):
    a = _dot(ret_ref[...], wr_ref[...])
    d = _dot(att_ref[...], wd_ref[...])
    merged = (jax.nn.sigmoid(gr_ref[...].astype(jnp.float32)) * a
              + jax.nn.sigmoid(ga_ref[...].astype(jnp.float32)) * d)
    mix = _dot(merged.astype(_MXU_DTYPE), wo_ref[...])
    h = _layer_norm(ALPHA * x_ref[...] + mix, g_ref[...], b_ref[...])
    h_ref[...] = h
    lg_ref[...] = jnp.dot(h, rw_ref[...], preferred_element_type=jnp.float32,
                          precision=lax.Precision.HIGHEST) + rb_ref[...]


def _merge(ret, att, pv, x, wr, wd, wo, g, b, rw, rb, tm=256):
    T, D = x.shape
    row = lambda i: (i, 0)
    const = lambda i: (0, 0)
    n_gate = pv.shape[1] // D
    return pl.pallas_call(
        _merge_kernel,
        grid=(T // tm,),
        in_specs=[pl.BlockSpec((tm, ret.shape[1]), row),
                  pl.BlockSpec((tm, att.shape[1]), row),
                  pl.BlockSpec((tm, D), lambda i: (i, n_gate - 2)),
                  pl.BlockSpec((tm, D), lambda i: (i, n_gate - 1)),
                  pl.BlockSpec((tm, D), row),
                  pl.BlockSpec(wr.shape, const),
                  pl.BlockSpec(wd.shape, const),
                  pl.BlockSpec(wo.shape, const),
                  pl.BlockSpec((1, D), const),
                  pl.BlockSpec((1, D), const),
                  pl.BlockSpec(rw.shape, const),
                  pl.BlockSpec((1, LANES), const)],
        out_specs=[pl.BlockSpec((tm, D), row), pl.BlockSpec((tm, LANES), row)],
        out_shape=[jax.ShapeDtypeStruct((T, D), jnp.float32),
                   jax.ShapeDtypeStruct((T, LANES), jnp.float32)],
        compiler_params=_cparams(("parallel",)),
        name="merge_ln_router",
    )(ret, att, pv, pv, x, wr, wd, wo, g, b, rw, rb)


def _route_kernel(lg_ref, o_ref, cnt_ref, carry_s):
    tm = lg_ref.shape[0]

    @pl.when(pl.program_id(0) == 0)
    def _():
        carry_s[...] = jnp.zeros_like(carry_s)

    lg = lg_ref[...]
    lane = lax.broadcasted_iota(jnp.int32, (tm, LANES), 1).astype(jnp.float32)
    neg = -jnp.inf

    def first_max(v):
        m = jnp.max(v, axis=-1, keepdims=True)
        idx = jnp.min(jnp.where(v == m, lane, float(LANES)), axis=-1, keepdims=True)
        return m, idx

    gl = jnp.where(lane < N_GRP, lg, neg)
    gmax, gsel = first_max(gl)
    p_grp = 1.0 / jnp.sum(jnp.exp(gl - gmax), axis=-1, keepdims=True)
    lo = N_GRP + gsel * EXP_PER_GRP
    el = jnp.where((lane >= lo) & (lane < lo + EXP_PER_GRP), lg, neg)
    top1, i1 = first_max(el)
    top2, i2 = first_max(jnp.where(lane == i1, neg, el))
    z = jnp.exp(top2 - top1)
    g1 = p_grp / (1.0 + z)
    g2 = p_grp * z / (1.0 + z)
    oh1 = lane == i1
    oh2 = lane == i2
    oh = (oh1 | oh2).astype(jnp.float32)
    r_i = lax.broadcasted_iota(jnp.int32, (tm, tm), 0)
    c_i = lax.broadcasted_iota(jnp.int32, (tm, tm), 1)
    lower = (r_i > c_i).astype(_MXU_DTYPE)
    before = _dot(lower, oh.astype(_MXU_DTYPE)) + carry_s[0:1, :]
    rank1 = jnp.sum(jnp.where(oh1, before, 0.0), axis=-1, keepdims=True)
    rank2 = jnp.sum(jnp.where(oh2, before, 0.0), axis=-1, keepdims=True)
    new_carry = carry_s[0:1, :] + jnp.sum(oh, axis=0, keepdims=True)
    carry_s[...] = jnp.broadcast_to(new_carry, carry_s.shape)
    cnt_ref[...] = carry_s[...]
    e1 = i1 - N_GRP
    e2 = i2 - N_GRP
    out = jnp.zeros((tm, LANES), jnp.float32)
    for col, val in enumerate((e1, e2, rank1, rank2, g1, g2)):
        out = jnp.where(lane == col, val, out)
    o_ref[...] = out


def _route(logits, tm=256):
    T = logits.shape[0]
    return pl.pallas_call(
        _route_kernel,
        grid=(T // tm,),
        in_specs=[pl.BlockSpec((tm, LANES), lambda i: (i, 0))],
        out_specs=[pl.BlockSpec((tm, LANES), lambda i: (i, 0)),
                   pl.BlockSpec((8, LANES), lambda i: (0, 0))],
        out_shape=[jax.ShapeDtypeStruct((T, LANES), jnp.float32),
                   jax.ShapeDtypeStruct((8, LANES), jnp.float32)],
        scratch_shapes=[pltpu.VMEM((8, LANES), jnp.float32)],
        compiler_params=_cparams(("arbitrary",)),
        name="route",
    )(logits)


def _row_copy(src_hbm, dst_vmem, sem, src_row, dst_row):
    return pltpu.make_async_copy(src_hbm.at[pl.ds(src_row, 1)], dst_vmem.at[pl.ds(dst_row, 1)], sem)


def _gather_rows(idx_ref, src_hbm, dst_vmem, sem, n, stride=1, offset=0):
    def issue(r, c):
        _row_copy(src_hbm, dst_vmem, sem, idx_ref[0, 0, offset + r * stride], r).start()
        return c

    lax.fori_loop(0, n, issue, 0)

    def drain(r, c):
        _row_copy(src_hbm, dst_vmem, sem, 0, r).wait()
        return c

    lax.fori_loop(0, n, drain, 0)


def _ffn_kernel(be_ref, tok_ref, h_hbm, wg_ref, wu_ref, wd_ref, y_ref, x_buf, sem):
    del be_ref
    _gather_rows(tok_ref, h_hbm, x_buf, sem, DISP_BLK)
    xb = x_buf[...].astype(_MXU_DTYPE)
    gate = _dot(xb, wg_ref[0])
    up = _dot(xb, wu_ref[0])
    act = (gate * jax.nn.sigmoid(gate)) * up
    y_ref[...] = _dot(act.astype(_MXU_DTYPE), wd_ref[0])


def _ffn(block_expert, tok_pad, h, wg, wu, wd):
    n_blocks = block_expert.shape[0]
    D, F = wg.shape[1], wg.shape[2]
    wmap = lambda i, be: (be[i], 0, 0)
    grid_spec = pltpu.PrefetchScalarGridSpec(
        num_scalar_prefetch=1,
        grid=(n_blocks,),
        in_specs=[pl.BlockSpec((1, 1, DISP_BLK), lambda i, be: (i, 0, 0), memory_space=pltpu.SMEM),
                  pl.BlockSpec(memory_space=pl.ANY),
                  pl.BlockSpec((1, D, F), wmap),
                  pl.BlockSpec((1, D, F), wmap),
                  pl.BlockSpec((1, F, D), wmap)],
        out_specs=pl.BlockSpec((DISP_BLK, D), lambda i, be: (i, 0)),
        scratch_shapes=[pltpu.VMEM((DISP_BLK, D), jnp.float32), pltpu.SemaphoreType.DMA(())],
    )
    return pl.pallas_call(
        _ffn_kernel,
        grid_spec=grid_spec,
        out_shape=jax.ShapeDtypeStruct((n_blocks * DISP_BLK, D), jnp.float32),
        compiler_params=_cparams(("arbitrary",)),
        name="expert_ffn",
    )(block_expert, tok_pad.reshape(n_blocks, 1, DISP_BLK), h, wg, wu, wd)


def _combine_kernel(dest_ref, y_hbm, r_ref, h_ref, g_ref, b_ref, o_ref, ob_ref, y0_buf, y1_buf, sem):
    tm = h_ref.shape[0]
    _gather_rows(dest_ref, y_hbm, y0_buf, sem, tm, stride=2, offset=0)
    _gather_rows(dest_ref, y_hbm, y1_buf, sem, tm, stride=2, offset=1)
    r = r_ref[...]
    ffn = y0_buf[...] * r[:, 4:5] + y1_buf[...] * r[:, 5:6]
    out = _layer_norm(ALPHA * h_ref[...] + ffn, g_ref[...], b_ref[...])
    o_ref[...] = out
    ob_ref[...] = out.astype(ob_ref.dtype)


def _combine(dest, y_pad, route, h, g, b, tm=128):
    T, D = h.shape
    row = lambda i: (i, 0)
    const = lambda i: (0, 0)
    return pl.pallas_call(
        _combine_kernel,
        grid=(T // tm,),
        in_specs=[pl.BlockSpec((1, 1, 2 * tm), lambda i: (i, 0, 0), memory_space=pltpu.SMEM),
                  pl.BlockSpec(memory_space=pl.ANY),
                  pl.BlockSpec((tm, LANES), row),
                  pl.BlockSpec((tm, D), row),
                  pl.BlockSpec((1, D), const),
                  pl.BlockSpec((1, D), const)],
        out_specs=[pl.BlockSpec((tm, D), row), pl.BlockSpec((tm, D), row)],
        out_shape=[jax.ShapeDtypeStruct((T, D), jnp.float32),
                   jax.ShapeDtypeStruct((T, D), _MXU_DTYPE)],
        scratch_shapes=[pltpu.VMEM((tm, D), jnp.float32), pltpu.VMEM((tm, D), jnp.float32),
                        pltpu.SemaphoreType.DMA(())],
        compiler_params=_cparams(("arbitrary",)),
        name="combine_ln",
    )(dest.reshape(T // tm, 1, 2 * tm), y_pad, route, h, g, b)


def _spread64(w):
    K, n = w.shape[0], w.shape[1] // IDX_D
    w = w.reshape(K, n, 2, IDX_D // 2)
    w = jnp.pad(w, ((0, 0), (0, 0), (0, 0), (0, IDX_D // 2)))
    return w.reshape(K, n * LANES)


def _split_w_in(w):
    sizes = (RET_HEADS * RET_DK, RET_HEADS * RET_DK, RET_HEADS * RET_DV, RET_HEADS * RET_DV,
             ATT_HEADS * ATT_HD, ATT_HD, ATT_HD, IDX_HEADS * IDX_D, IDX_D, IDX_HEADS,
             D_MODEL_, D_MODEL_)
    offs = np.cumsum((0,) + sizes)
    rq, rk, rv, rg, aq, ak, av, iq, ik, iw, gr, ga = [w[:, offs[i]:offs[i + 1]] for i in range(12)]
    c = lambda *a: jnp.concatenate(a, axis=1).astype(_MXU_DTYPE)
    w_qk = c(rq, rk)
    w_pv = c(rv, rg, gr, ga)
    w_aq = aq.astype(_MXU_DTYPE)
    w_iq = _spread64(iq).astype(_MXU_DTYPE)
    w_e = c(ak, av, _spread64(ik), jnp.pad(iw, ((0, 0), (0, LANES - IDX_HEADS))))
    return w_qk, w_pv, w_aq, w_iq, w_e


def _moe_plan(route, counts):
    T = route.shape[0]
    n_assign = T * 2
    n_blocks = -(-n_assign // DISP_BLK) + N_EXP
    cnt = counts[0, N_GRP:N_GRP + N_EXP].astype(jnp.int32)
    padded = (cnt + DISP_BLK - 1) // DISP_BLK * DISP_BLK
    pad_ends = jnp.cumsum(padded)
    pad_starts = pad_ends - padded
    expert = route[:, 0:2].astype(jnp.int32)
    rank = route[:, 2:4].astype(jnp.int32)
    dest = (pad_starts[expert] + rank).reshape(n_assign)
    tok = jnp.repeat(jnp.arange(T, dtype=jnp.int32), 2)
    tok_pad = jnp.zeros((n_blocks * DISP_BLK,), jnp.int32).at[dest].set(tok)
    block_expert = jnp.minimum(
        jnp.searchsorted(pad_ends, jnp.arange(n_blocks, dtype=jnp.int32) * DISP_BLK, side="right"),
        N_EXP - 1).astype(jnp.int32)
    return dest, tok_pad, block_expert


def kernel(x, w_in, w_ret_o, w_dsa_o, w_out, ln1_g, ln1_b, router_grp_w, router_grp_b,
           router_exp_w, router_exp_b, w_gate, w_up, w_down, ln2_g, ln2_b):
    B, S, D = x.shape
    T = B * S
    cdt = _MXU_DTYPE
    t_ret = [jnp.asarray(t) for t in _rot_tables(S, RET_DK, 2 * RET_DK, "half")]
    t_att = [jnp.asarray(t) for t in _rot_tables(S, ATT_HD, 4 * ATT_HD, "half")]
    t_idx = [jnp.asarray(t) for t in _rot_tables(S, IDX_D, 4 * LANES, "spread")]
    ident = (np.ones((S, LANES), np.float32), np.zeros((S, LANES), np.float32))
    t_e = [jnp.asarray(np.concatenate([ta[:, :LANES], i, ti[:, :LANES], i], axis=1))
           for ta, ti, i in zip(_rot_tables(S, ATT_HD, LANES, "half"),
                                _rot_tables(S, IDX_D, LANES, "spread"), ident)]
    xf = x.reshape(T, D)
    xb = xf.astype(cdt)
    for l in range(DEPTH_):
        w_qk, w_pv, w_aq, w_iq, w_e = _split_w_in(w_in[l])
        qk = _proj(xb, w_qk, S, rot=RET_DK, tables=t_ret)
        pv = _proj(xb, w_pv, S)
        aq = _proj(xb, w_aq, S, rot=ATT_HD, tables=t_att)
        iq = _proj(xb, w_iq, S, rot=LANES, tables=t_idx)
        e = _proj(xb, w_e, S, rot=LANES, tables=t_e, out_dtype=jnp.float32)
        ret = _retention(qk, pv, B, S)
        att = _dsa(aq, iq, e, B, S)
        rw = jnp.pad(jnp.concatenate([router_grp_w[l], router_exp_w[l]], axis=1),
                     ((0, 0), (0, LANES - N_GRP - N_EXP)))
        rb = jnp.pad(jnp.concatenate([router_grp_b[l], router_exp_b[l]]),
                     (0, LANES - N_GRP - N_EXP)).reshape(1, LANES)
        h, logits = _merge(ret, att, pv, xf, w_ret_o[l].astype(cdt), w_dsa_o[l].astype(cdt),
                           w_out[l].astype(cdt), ln1_g[l].reshape(1, D), ln1_b[l].reshape(1, D),
                           rw, rb)
        route, counts = _route(logits)
        dest, tok_pad, block_expert = _moe_plan(route, counts)
        y_pad = _ffn(block_expert, tok_pad, h, w_gate[l].astype(cdt), w_up[l].astype(cdt),
                     w_down[l].astype(cdt))
        xf, xb = _combine(dest, y_pad, route, h, ln2_g[l].reshape(1, D), ln2_b[l].reshape(1, D))
    return xf.reshape(B, S, D)
```

```python
import functools

import numpy as np
import jax
import jax.numpy as jnp
from jax import lax
from jax.experimental import pallas as pl
from jax.experimental.pallas import tpu as pltpu

D_MODEL_ = 1024
DEPTH_ = 2
RET_HEADS = 4
RET_DK = 256
RET_DV = 512
RET_C = 128
ATT_HEADS = 8
ATT_HD = 128
IDX_HEADS = 8
IDX_D = 64
TOPK = 256
QBLK = 128
N_GRP = 4
EXP_PER_GRP = 8
N_EXP = N_GRP * EXP_PER_GRP
EXP_FF = 512
DISP_BLK = 128
THETA = 10000.0
LN_EPS_ = 1e-5
GN_EPS_ = 1e-5
ALPHA = (2 * DEPTH_) ** 0.25

LANES = 128
INT_MIN = -(2 ** 31)
_CODE_NEG_INF = INT_MIN + 0x7FFFFF
_F32_LOWEST = float(np.finfo(np.float32).min)

_MXU_DTYPE = jnp.bfloat16
_VMEM_LIMIT = 56 * 1024 * 1024


def _cparams(sem):
    return pltpu.CompilerParams(dimension_semantics=sem, vmem_limit_bytes=_VMEM_LIMIT)


def _dot(a, b):
    return jnp.dot(a, b, preferred_element_type=jnp.float32)


def _dot_nt(a, b):
    return lax.dot_general(a, b, (((1,), (1,)), ((), ())), preferred_element_type=jnp.float32)


def _dot_tn(a, b):
    return lax.dot_general(a, b, (((0,), (0,)), ((), ())), preferred_element_type=jnp.float32)


def _swap_halves(a, d):
    tn = a.shape[1]
    if d == 2 * LANES:
        parts = []
        for c in range(0, tn, d):
            parts += [a[:, c + LANES:c + d], a[:, c:c + LANES]]
        return jnp.concatenate(parts, axis=1)
    assert d == LANES
    return jnp.concatenate(
        [pltpu.roll(a[:, c:c + LANES], LANES // 2, axis=1) for c in range(0, tn, LANES)], axis=1)


def _proj_kernel(x_ref, w_ref, *rest, rot):
    if rot:
        c_ref, s_ref, o_ref = rest
    else:
        (o_ref,) = rest
    acc = _dot(x_ref[...], w_ref[...])
    if rot:
        acc = acc * c_ref[...] + _swap_halves(acc, rot) * s_ref[...]
    o_ref[...] = acc.astype(o_ref.dtype)


def _proj(x, w, seq, *, rot=0, tables=None, out_dtype=None, tm=1024, tn=512):
    T, K = x.shape
    N = w.shape[1]
    tn = min(tn, N)
    tm = min(tm, seq)
    n_seq = seq // tm
    in_specs = [pl.BlockSpec((tm, K), lambda i, j: (i, 0)),
                pl.BlockSpec((K, tn), lambda i, j: (0, j))]
    args = [x, w]
    if rot:
        in_specs += [pl.BlockSpec((tm, tn), lambda i, j: (i % n_seq, 0))] * 2
        args += list(tables)
    return pl.pallas_call(
        functools.partial(_proj_kernel, rot=rot),
        grid=(T // tm, N // tn),
        in_specs=in_specs,
        out_specs=pl.BlockSpec((tm, tn), lambda i, j: (i, j)),
        out_shape=jax.ShapeDtypeStruct((T, N), out_dtype or x.dtype),
        compiler_params=_cparams(("parallel", "parallel")),
        name="proj_rot%d" % rot,
    )(*args)


def _rot_tables(seq, d, width, layout):
    inv = 1.0 / (THETA ** (np.arange(0, d, 2, dtype=np.float64) / d))
    ang = np.arange(seq, dtype=np.float64)[:, None] * inv[None, :]
    cos, sin = np.cos(ang), np.sin(ang)
    if layout == "half":
        c = np.concatenate([cos, cos], axis=1)
        s = np.concatenate([-sin, sin], axis=1)
    else:
        z = np.zeros_like(cos)
        c = np.concatenate([cos, z, cos, z], axis=1)
        s = np.concatenate([-sin, z, sin, z], axis=1)
    reps = width // c.shape[1]
    return np.tile(c, (1, reps)).astype(np.float32), np.tile(s, (1, reps)).astype(np.float32)


def _ret_kernel(qk_ref, v_ref, g_ref, dec_ref, xi_ref, zeta_ref, o_ref, r_ref, *, gamma_c):
    H, DK, DV = RET_HEADS, RET_DK, RET_DV

    @pl.when(pl.program_id(1) == 0)
    def _():
        r_ref[...] = jnp.zeros_like(r_ref)

    for h in range(H):
        q = qk_ref[:, h * DK:(h + 1) * DK]
        k = qk_ref[:, (H + h) * DK:(H + h + 1) * DK]
        v = v_ref[:, h * DV:(h + 1) * DV]
        s = _dot_nt(q, k) * dec_ref[h]
        intra = _dot(s.astype(_MXU_DTYPE), v)
        r_old = r_ref[h]
        cross = _dot(q, r_old.astype(_MXU_DTYPE))
        xi = jnp.concatenate([xi_ref[h]] * (DV // LANES), axis=1)
        zeta = jnp.concatenate([zeta_ref[h]] * (DV // LANES), axis=1)
        y = intra + cross * xi
        vz = (v.astype(jnp.float32) * zeta).astype(_MXU_DTYPE)
        r_ref[h] = gamma_c[h] * r_old + _dot_tn(k, vz)
        mu = jnp.mean(y, axis=-1, keepdims=True)
        yc = y - mu
        var = jnp.mean(yc * yc, axis=-1, keepdims=True)
        yn = yc * lax.rsqrt(var + GN_EPS_)
        g = g_ref[:, h * DV:(h + 1) * DV].astype(jnp.float32)
        o_ref[:, h * DV:(h + 1) * DV] = (yn * (g * jax.nn.sigmoid(g))).astype(o_ref.dtype)


def _retention(qk, pv, batch, seq):
    H, DK, DV, C = RET_HEADS, RET_DK, RET_DV, RET_C
    n_chunks = seq // C
    lg = np.log(1.0 - 2.0 ** (-5.0 - np.arange(H, dtype=np.float64)))
    pos = np.arange(C, dtype=np.float64)
    diff = pos[:, None] - pos[None, :]
    dec = np.where(diff >= 0, np.exp(np.maximum(diff, 0.0)[None] * lg[:, None, None]), 0.0)
    dec = (dec * DK ** -0.5).astype(np.float32)
    ones = np.ones((1, 1, LANES))
    xi = (np.exp((pos[None, :] + 1.0) * lg[:, None])[:, :, None] * ones).astype(np.float32)
    zeta = (np.exp((C - 1.0 - pos)[None, :] * lg[:, None])[:, :, None] * ones
            * DK ** -0.5).astype(np.float32)
    gamma_c = tuple(float(g) for g in np.exp(C * lg))
    T = qk.shape[0]
    const3 = lambda b, n: (0, 0, 0)
    row = lambda b, n: (b * n_chunks + n, 0)
    return pl.pallas_call(
        functools.partial(_ret_kernel, gamma_c=gamma_c),
        grid=(batch, n_chunks),
        in_specs=[pl.BlockSpec((C, 2 * H * DK), row),
                  pl.BlockSpec((C, H * DV), row),
                  pl.BlockSpec((C, H * DV), lambda b, n: (b * n_chunks + n, 1)),
                  pl.BlockSpec((H, C, C), const3),
                  pl.BlockSpec((H, C, LANES), const3),
                  pl.BlockSpec((H, C, LANES), const3)],
        out_specs=pl.BlockSpec((C, H * DV), row),
        out_shape=jax.ShapeDtypeStruct((T, H * DV), qk.dtype),
        scratch_shapes=[pltpu.VMEM((H, DK, DV), jnp.float32)],
        compiler_params=_cparams(("parallel", "arbitrary")),
        name="retention",
    )(qk, pv, pv, jnp.asarray(dec), jnp.asarray(xi), jnp.asarray(zeta))


_COUNT_CHAINS = 8
_DSA_CLASSES = 4


def _col_count(mask):
    n, qb = mask.shape
    part = jnp.sum(mask.astype(jnp.int32).reshape(_COUNT_CHAINS, n // _COUNT_CHAINS, qb), axis=1)
    return jnp.sum(part, axis=0, keepdims=True)


def _dsa_block(sv, j, aq_ref, iq_ref, eq_ref, o_ref, k_s, ik_s, vt_s, score_s, bias_s, topk):
    QB = aq_ref.shape[0]
    w_t = eq_ref[:, 3 * LANES:4 * LANES].T * ((IDX_D ** -0.5) * (IDX_HEADS ** -0.5))
    score = jnp.zeros((sv, QB), jnp.float32)
    for h in range(IDX_HEADS):
        rel = jnp.maximum(_dot_nt(ik_s[0:sv, :], iq_ref[:, h * LANES:(h + 1) * LANES]), 0.0)
        score = score + rel * w_t[h:h + 1, :]
    key_pos = lax.broadcasted_iota(jnp.int32, (sv, QB), 0)
    q_pos = j * QB + lax.broadcasted_iota(jnp.int32, (sv, QB), 1)
    score_s[0:sv, :] = jnp.where(key_pos <= q_pos, score, -jnp.inf)

    def count_ge(c):
        return _col_count(score_s[0:sv, :] >= c)

    def float_of(code):
        return pltpu.bitcast(jnp.where(code < 0, code ^ 0x7FFFFFFF, code), jnp.float32)

    code = jnp.where(count_ge(jnp.zeros((1, QB), jnp.float32)) >= topk, 0, INT_MIN)

    def bit_step(i, code):
        cand = code | lax.shift_left(jnp.int32(1), 30 - i)
        ok = (count_ge(float_of(cand)) >= topk) | (cand <= _CODE_NEG_INF)
        return jnp.where(ok, cand, code)

    code = lax.fori_loop(0, 31, bit_step, code)
    thr = float_of(code)
    thr_lo = jnp.maximum(thr, _F32_LOWEST)
    has_tie = jnp.max(jnp.where(code > _CODE_NEG_INF, count_ge(thr) - topk, 0)) > 0

    @pl.when(jnp.logical_not(has_tie))
    def _():
        bias_s[0:sv, :] = jnp.where(score_s[0:sv, :] >= thr_lo, 0.0, -jnp.inf)

    @pl.when(has_tie)
    def _():
        need_f = (topk - _col_count(score_s[0:sv, :] > thr)).astype(jnp.float32)
        r_i = lax.broadcasted_iota(jnp.int32, (LANES, LANES), 0)
        c_i = lax.broadcasted_iota(jnp.int32, (LANES, LANES), 1)
        lower = (r_i > c_i).astype(_MXU_DTYPE)
        carry = jnp.zeros((1, QB), jnp.float32)
        for c in range(sv // LANES):
            sl = slice(c * LANES, (c + 1) * LANES)
            sc = score_s[sl, :]
            eq_c = sc == thr
            eq_f = eq_c.astype(jnp.float32)
            before = _dot(lower, eq_f.astype(_MXU_DTYPE)) + carry
            take = ((sc > thr) | (eq_c & (before < need_f))) & (sc >= _F32_LOWEST)
            bias_s[sl, :] = jnp.where(take, 0.0, -jnp.inf)
            carry = carry + jnp.sum(eq_f, axis=0, keepdims=True)

    c_exp = (ATT_HD ** -0.5) * float(np.log2(np.e))
    for h in range(ATT_HEADS):
        s = _dot_nt(k_s[0:sv, :], aq_ref[:, h * ATT_HD:(h + 1) * ATT_HD]) + bias_s[0:sv, :]
        m = jnp.max(s, axis=0, keepdims=True)
        p = jnp.exp2((s - m) * c_exp)
        l = jnp.sum(p, axis=0, keepdims=True)
        o_t = _dot(vt_s[:, 0:sv], p.astype(_MXU_DTYPE)) / l
        o_ref[:, h * ATT_HD:(h + 1) * ATT_HD] = o_t.T.astype(o_ref.dtype)


def _dsa_kernel(aq_ref, iq_ref, eq_ref, ek_ref, o_ref, k_s, ik_s, vt_s, score_s, bias_s, *, topk):
    S = ek_ref.shape[0]
    QB = aq_ref.shape[0]
    j = pl.program_id(1)

    @pl.when(j == 0)
    def _():
        k_s[...] = ek_ref[:, 0:LANES].astype(k_s.dtype)
        vt_s[...] = ek_ref[:, LANES:2 * LANES].T.astype(vt_s.dtype)
        ik_s[...] = ek_ref[:, 2 * LANES:3 * LANES].astype(ik_s.dtype)

    per_class = S // QB // _DSA_CLASSES
    for c in range(_DSA_CLASSES):
        @pl.when(j // per_class == c)
        def _(c=c):
            _dsa_block((c + 1) * per_class * QB, j, aq_ref, iq_ref, eq_ref, o_ref,
                       k_s, ik_s, vt_s, score_s, bias_s, topk)


def _dsa(aq, iq, e, batch, seq):
    T = aq.shape[0]
    nb = seq // QBLK
    topk = min(TOPK, seq // 4)
    qrow = lambda b, j: (b * nb + j, 0)
    return pl.pallas_call(
        functools.partial(_dsa_kernel, topk=topk),
        grid=(batch, nb),
        in_specs=[pl.BlockSpec((QBLK, ATT_HEADS * ATT_HD), qrow),
                  pl.BlockSpec((QBLK, IDX_HEADS * LANES), qrow),
                  pl.BlockSpec((QBLK, 4 * LANES), qrow),
                  pl.BlockSpec((seq, 4 * LANES), lambda b, j: (b, 0))],
        out_specs=pl.BlockSpec((QBLK, ATT_HEADS * ATT_HD), qrow),
        out_shape=jax.ShapeDtypeStruct((T, ATT_HEADS * ATT_HD), aq.dtype),
        scratch_shapes=[pltpu.VMEM((seq, LANES), aq.dtype),
                        pltpu.VMEM((seq, LANES), aq.dtype),
                        pltpu.VMEM((LANES, seq), aq.dtype),
                        pltpu.VMEM((seq, QBLK), jnp.float32),
                        pltpu.VMEM((seq, QBLK), jnp.float32)],
        compiler_params=_cparams(("parallel", "arbitrary")),
        name="dsa",
    )(aq, iq, e, e)


def _layer_norm(z, g, b):
    mu = jnp.mean(z, axis=-1, keepdims=True)
    zc = z - mu
    var = jnp.mean(zc * zc, axis=-1, keepdims=True)
    return zc * lax.rsqrt(var + LN_EPS_) * g + b


def _merge_kernel(ret_ref, att_ref, gr_ref, ga_ref, x_ref, wr_ref, wd_ref, wo_ref, g_ref, b_ref,
                  rw_ref, rb_ref, h_ref, lg_ref):
    a = _dot(ret_ref[...], wr_ref[...])
    d = _dot(att_ref[...], wd_ref[...])
    merged = (jax.nn.sigmoid(gr_ref[...].astype(jnp.float32)) * a
              + jax.nn.sigmoid(ga_ref[...].astype(jnp.float32)) * d)
    mix = _dot(merged.astype(_MXU_DTYPE), wo_ref[...])
    h = _layer_norm(ALPHA * x_ref[...] + mix, g_ref[...], b_ref[...])
    h_ref[...] = h
    lg_ref[...] = jnp.dot(h, rw_ref[...], preferred_element_type=jnp.float32,
                          precision=lax.Precision.HIGHEST) + rb_ref[...]


def _merge(ret, att, pv, x, wr, wd, wo, g, b, rw, rb, tm=256):
    T, D = x.shape
    row = lambda i: (i, 0)
    const = lambda i: (0, 0)
    n_gate = pv.shape[1] // D
    return pl.pallas_call(
        _merge_kernel,
        grid=(T // tm,),
        in_specs=[pl.BlockSpec((tm, ret.shape[1]), row),
                  pl.BlockSpec((tm, att.shape[1]), row),
                  pl.BlockSpec((tm, D), lambda i: (i, n_gate - 2)),
                  pl.BlockSpec((tm, D), lambda i: (i, n_gate - 1)),
                  pl.BlockSpec((tm, D), row),
                  pl.BlockSpec(wr.shape, const),
                  pl.BlockSpec(wd.shape, const),
                  pl.BlockSpec(wo.shape, const),
                  pl.BlockSpec((1, D), const),
                  pl.BlockSpec((1, D), const),
                  pl.BlockSpec(rw.shape, const),
                  pl.BlockSpec((1, LANES), const)],
        out_specs=[pl.BlockSpec((tm, D), row), pl.BlockSpec((tm, LANES), row)],
        out_shape=[jax.ShapeDtypeStruct((T, D), jnp.float32),
                   jax.ShapeDtypeStruct((T, LANES), jnp.float32)],
        compiler_params=_cparams(("parallel",)),
        name="merge_ln_router",
    )(ret, att, pv, pv, x, wr, wd, wo, g, b, rw, rb)


def _route_kernel(lg_ref, o_ref, cnt_ref, carry_s):
    tm = lg_ref.shape[0]

    @pl.when(pl.program_id(0) == 0)
    def _():
        carry_s[...] = jnp.zeros_like(carry_s)

    lg = lg_ref[...]
    lane = lax.broadcasted_iota(jnp.int32, (tm, LANES), 1).astype(jnp.float32)
    neg = -jnp.inf

    def first_max(v):
        m = jnp.max(v, axis=-1, keepdims=True)
        idx = jnp.min(jnp.where(v == m, lane, float(LANES)), axis=-1, keepdims=True)
        return m, idx

    gl = jnp.where(lane < N_GRP, lg, neg)
    gmax, gsel = first_max(gl)
    p_grp = 1.0 / jnp.sum(jnp.exp(gl - gmax), axis=-1, keepdims=True)
    lo = N_GRP + gsel * EXP_PER_GRP
    el = jnp.where((lane >= lo) & (lane < lo + EXP_PER_GRP), lg, neg)
    top1, i1 = first_max(el)
    top2, i2 = first_max(jnp.where(lane == i1, neg, el))
    z = jnp.exp(top2 - top1)
    g1 = p_grp / (1.0 + z)
    g2 = p_grp * z / (1.0 + z)
    oh1 = lane == i1
    oh2 = lane == i2
    oh = (oh1 | oh2).astype(jnp.float32)
    r_i = lax.broadcasted_iota(jnp.int32, (tm, tm), 0)
    c_i = lax.broadcasted_iota(jnp.int32, (tm, tm), 1)
    lower = (r_i > c_i).astype(_MXU_DTYPE)
    before = _dot(lower, oh.astype(_MXU_DTYPE)) + carry_s[0:1, :]
    rank1 = jnp.sum(jnp.where(oh1, before, 0.0), axis=-1, keepdims=True)
    rank2 = jnp.sum(jnp.where(oh2, before, 0.0), axis=-1, keepdims=True)
    new_carry = carry_s[0:1, :] + jnp.sum(oh, axis=0, keepdims=True)
    carry_s[...] = jnp.broadcast_to(new_carry, carry_s.shape)
    cnt_ref[...] = carry_s[...]
    e1 = i1 - N_GRP
    e2 = i2 - N_GRP
    out = jnp.zeros((tm, LANES), jnp.float32)
    for col, val in enumerate((e1, e2, rank1, rank2, g1, g2)):
        out = jnp.where(lane == col, val, out)
    o_ref[...] = out


def _route(logits, tm=256):
    T = logits.shape[0]
    return pl.pallas_call(
        _route_kernel,
        grid=(T // tm,),
        in_specs=[pl.BlockSpec((tm, LANES), lambda i: (i, 0))],
        out_specs=[pl.BlockSpec((tm, LANES), lambda i: (i, 0)),
                   pl.BlockSpec((8, LANES), lambda i: (0, 0))],
        out_shape=[jax.ShapeDtypeStruct((T, LANES), jnp.float32),
                   jax.ShapeDtypeStruct((8, LANES), jnp.float32)],
        scratch_shapes=[pltpu.VMEM((8, LANES), jnp.float32)],
        compiler_params=_cparams(("arbitrary",)),
        name="route",
    )(logits)


def _row_copy(src_hbm, dst_vmem, sem, src_row, dst_row):
    return pltpu.make_async_copy(src_hbm.at[pl.ds(src_row, 1)], dst_vmem.at[pl.ds(dst_row, 1)], sem)


_GATHER_UNROLL = 8


def _start_row_gather(idx_ref, src_hbm, dst_vmem, sem, n, stride=1, offset=0):
    def issue(r, c):
        _row_copy(src_hbm, dst_vmem, sem, idx_ref[0, 0, offset + r * stride], r).start()
        return c

    lax.fori_loop(0, n, issue, 0, unroll=_GATHER_UNROLL)


def _wait_row_gather(src_hbm, dst_vmem, sem, n):
    pltpu.make_async_copy(src_hbm.at[pl.ds(0, n)], dst_vmem, sem).wait()


def _ffn_kernel(be_ref, tok_ref, tok_next_ref, h_hbm, wg_ref, wu_ref, wd_ref, y_ref,
                x_buf, wg_s, wu_s, wd_s, sem):
    i = pl.program_id(0)
    n = pl.num_programs(0)
    slot = i % 2

    @pl.when(i == 0)
    def _():
        _start_row_gather(tok_ref, h_hbm, x_buf.at[0], sem.at[0], DISP_BLK)

    @pl.when(i + 1 < n)
    def _():
        _start_row_gather(tok_next_ref, h_hbm, x_buf.at[1 - slot], sem.at[1 - slot], DISP_BLK)

    @pl.when((i == 0) | (be_ref[i] != be_ref[jnp.maximum(i - 1, 0)]))
    def _():
        wg_s[...] = wg_ref[0].astype(wg_s.dtype)
        wu_s[...] = wu_ref[0].astype(wu_s.dtype)
        wd_s[...] = wd_ref[0].astype(wd_s.dtype)

    _wait_row_gather(h_hbm, x_buf.at[slot], sem.at[slot], DISP_BLK)
    xb = x_buf[slot].astype(_MXU_DTYPE)
    gate = _dot(xb, wg_s[...])
    up = _dot(xb, wu_s[...])
    act = (gate * jax.nn.sigmoid(gate)) * up
    y_ref[...] = _dot(act.astype(_MXU_DTYPE), wd_s[...])


def _ffn(block_expert, tok_pad, h, wg, wu, wd):
    n_blocks = block_expert.shape[0]
    D, F = wg.shape[1], wg.shape[2]
    wmap = lambda i, be: (be[i], 0, 0)
    tok3 = tok_pad.reshape(n_blocks, 1, DISP_BLK)
    grid_spec = pltpu.PrefetchScalarGridSpec(
        num_scalar_prefetch=1,
        grid=(n_blocks,),
        in_specs=[pl.BlockSpec((1, 1, DISP_BLK), lambda i, be: (i, 0, 0), memory_space=pltpu.SMEM),
                  pl.BlockSpec((1, 1, DISP_BLK), lambda i, be: (jnp.minimum(i + 1, n_blocks - 1), 0, 0),
                               memory_space=pltpu.SMEM),
                  pl.BlockSpec(memory_space=pl.ANY),
                  pl.BlockSpec((1, D, F), wmap),
                  pl.BlockSpec((1, D, F), wmap),
                  pl.BlockSpec((1, F, D), wmap)],
        out_specs=pl.BlockSpec((DISP_BLK, D), lambda i, be: (i, 0)),
        scratch_shapes=[pltpu.VMEM((2, DISP_BLK, D), jnp.float32),
                        pltpu.VMEM((D, F), _MXU_DTYPE),
                        pltpu.VMEM((D, F), _MXU_DTYPE),
                        pltpu.VMEM((F, D), _MXU_DTYPE),
                        pltpu.SemaphoreType.DMA((2,))],
    )
    return pl.pallas_call(
        _ffn_kernel,
        grid_spec=grid_spec,
        out_shape=jax.ShapeDtypeStruct((n_blocks * DISP_BLK, D), jnp.float32),
        compiler_params=_cparams(("arbitrary",)),
        name="expert_ffn",
    )(block_expert, tok3, tok3, h, wg, wu, wd)


def _combine_kernel(dest_ref, dest_next_ref, y_hbm, r_ref, h_ref, g_ref, b_ref, o_ref, ob_ref,
                    y_buf, sem):
    tm = h_ref.shape[0]
    i = pl.program_id(0)
    n = pl.num_programs(0)
    slot = i % 2

    def start(idx_ref, s):
        for c in range(2):
            _start_row_gather(idx_ref, y_hbm, y_buf.at[s, c], sem.at[s, c], tm, stride=2, offset=c)

    @pl.when(i == 0)
    def _():
        start(dest_ref, 0)

    @pl.when(i + 1 < n)
    def _():
        start(dest_next_ref, 1 - slot)

    for c in range(2):
        _wait_row_gather(y_hbm, y_buf.at[slot, c], sem.at[slot, c], tm)
    r = r_ref[...]
    ffn = y_buf[slot, 0] * r[:, 4:5] + y_buf[slot, 1] * r[:, 5:6]
    out = _layer_norm(ALPHA * h_ref[...] + ffn, g_ref[...], b_ref[...])
    o_ref[...] = out
    ob_ref[...] = out.astype(ob_ref.dtype)


def _combine(dest, y_pad, route, h, g, b, tm=128):
    T, D = h.shape
    n = T // tm
    row = lambda i: (i, 0)
    const = lambda i: (0, 0)
    dest3 = dest.reshape(n, 1, 2 * tm)
    return pl.pallas_call(
        _combine_kernel,
        grid=(n,),
        in_specs=[pl.BlockSpec((1, 1, 2 * tm), lambda i: (i, 0, 0), memory_space=pltpu.SMEM),
                  pl.BlockSpec((1, 1, 2 * tm), lambda i: (jnp.minimum(i + 1, n - 1), 0, 0),
                               memory_space=pltpu.SMEM),
                  pl.BlockSpec(memory_space=pl.ANY),
                  pl.BlockSpec((tm, LANES), row),
                  pl.BlockSpec((tm, D), row),
                  pl.BlockSpec((1, D), const),
                  pl.BlockSpec((1, D), const)],
        out_specs=[pl.BlockSpec((tm, D), row), pl.BlockSpec((tm, D), row)],
        out_shape=[jax.ShapeDtypeStruct((T, D), jnp.float32),
                   jax.ShapeDtypeStruct((T, D), _MXU_DTYPE)],
        scratch_shapes=[pltpu.VMEM((2, 2, tm, D), jnp.float32),
                        pltpu.SemaphoreType.DMA((2, 2))],
        compiler_params=_cparams(("arbitrary",)),
        name="combine_ln",
    )(dest3, dest3, y_pad, route, h, g, b)


def _spread64(w):
    K, n = w.shape[0], w.shape[1] // IDX_D
    w = w.reshape(K, n, 2, IDX_D // 2)
    w = jnp.pad(w, ((0, 0), (0, 0), (0, 0), (0, IDX_D // 2)))
    return w.reshape(K, n * LANES)


def _split_w_in(w):
    sizes = (RET_HEADS * RET_DK, RET_HEADS * RET_DK, RET_HEADS * RET_DV, RET_HEADS * RET_DV,
             ATT_HEADS * ATT_HD, ATT_HD, ATT_HD, IDX_HEADS * IDX_D, IDX_D, IDX_HEADS,
             D_MODEL_, D_MODEL_)
    offs = np.cumsum((0,) + sizes)
    rq, rk, rv, rg, aq, ak, av, iq, ik, iw, gr, ga = [w[:, offs[i]:offs[i + 1]] for i in range(12)]
    c = lambda *a: jnp.concatenate(a, axis=1).astype(_MXU_DTYPE)
    w_qk = c(rq, rk)
    w_pv = c(rv, rg, gr, ga)
    w_aq = aq.astype(_MXU_DTYPE)
    w_iq = _spread64(iq).astype(_MXU_DTYPE)
    w_e = c(ak, av, _spread64(ik), jnp.pad(iw, ((0, 0), (0, LANES - IDX_HEADS))))
    return w_qk, w_pv, w_aq, w_iq, w_e


def _moe_plan(route, counts):
    T = route.shape[0]
    n_assign = T * 2
    n_blocks = -(-n_assign // DISP_BLK) + N_EXP
    cnt = counts[0, N_GRP:N_GRP + N_EXP].astype(jnp.int32)
    padded = (cnt + DISP_BLK - 1) // DISP_BLK * DISP_BLK
    pad_ends = jnp.cumsum(padded)
    pad_starts = pad_ends - padded
    expert = route[:, 0:2].astype(jnp.int32)
    rank = route[:, 2:4].astype(jnp.int32)
    dest = (pad_starts[expert] + rank).reshape(n_assign)
    tok = jnp.repeat(jnp.arange(T, dtype=jnp.int32), 2)
    tok_pad = jnp.zeros((n_blocks * DISP_BLK,), jnp.int32).at[dest].set(tok)
    block_start = jnp.arange(n_blocks, dtype=jnp.int32) * DISP_BLK
    block_expert = jnp.minimum(
        jnp.sum((pad_ends[None, :] <= block_start[:, None]).astype(jnp.int32), axis=1), N_EXP - 1)
    return dest, tok_pad, block_expert


def kernel(x, w_in, w_ret_o, w_dsa_o, w_out, ln1_g, ln1_b, router_grp_w, router_grp_b,
           router_exp_w, router_exp_b, w_gate, w_up, w_down, ln2_g, ln2_b):
    B, S, D = x.shape
    T = B * S
    cdt = _MXU_DTYPE
    t_ret = [jnp.asarray(t) for t in _rot_tables(S, RET_DK, 2 * RET_DK, "half")]
    t_att = [jnp.asarray(t) for t in _rot_tables(S, ATT_HD, 4 * ATT_HD, "half")]
    t_idx = [jnp.asarray(t) for t in _rot_tables(S, IDX_D, 4 * LANES, "spread")]
    ident = (np.ones((S, LANES), np.float32), np.zeros((S, LANES), np.float32))
    t_e = [jnp.asarray(np.concatenate([ta[:, :LANES], i, ti[:, :LANES], i], axis=1))
           for ta, ti, i in zip(_rot_tables(S, ATT_HD, LANES, "half"),
                                _rot_tables(S, IDX_D, LANES, "spread"), ident)]
    xf = x.reshape(T, D)
    xb = xf.astype(cdt)
    for l in range(DEPTH_):
        w_qk, w_pv, w_aq, w_iq, w_e = _split_w_in(w_in[l])
        qk = _proj(xb, w_qk, S, rot=RET_DK, tables=t_ret)
        pv = _proj(xb, w_pv, S)
        aq = _proj(xb, w_aq, S, rot=ATT_HD, tables=t_att)
        iq = _proj(xb, w_iq, S, rot=LANES, tables=t_idx)
        e = _proj(xb, w_e, S, rot=LANES, tables=t_e, out_dtype=jnp.float32)
        ret = _retention(qk, pv, B, S)
        att = _dsa(aq, iq, e, B, S)
        rw = jnp.pad(jnp.concatenate([router_grp_w[l], router_exp_w[l]], axis=1),
                     ((0, 0), (0, LANES - N_GRP - N_EXP)))
        rb = jnp.pad(jnp.concatenate([router_grp_b[l], router_exp_b[l]]),
                     (0, LANES - N_GRP - N_EXP)).reshape(1, LANES)
        h, logits = _merge(ret, att, pv, xf, w_ret_o[l].astype(cdt), w_dsa_o[l].astype(cdt),
                           w_out[l].astype(cdt), ln1_g[l].reshape(1, D), ln1_b[l].reshape(1, D),
                           rw, rb)
        route, counts = _route(logits)
        dest, tok_pad, block_expert = _moe_plan(route, counts)
        y_pad = _ffn(block_expert, tok_pad, h, w_gate[l], w_up[l], w_down[l])
        xf, xb = _combine(dest, y_pad, route, h, ln2_g[l].reshape(1, D), ln2_b[l].reshape(1, D))
    return xf.reshape(B, S, D)
```

```python
import functools

import numpy as np
import jax
import jax.numpy as jnp
from jax import lax
from jax.experimental import pallas as pl
from jax.experimental.pallas import tpu as pltpu

D_MODEL_ = 1024
DEPTH_ = 2
RET_HEADS = 4
RET_DK = 256
RET_DV = 512
RET_C = 128
ATT_HEADS = 8
ATT_HD = 128
IDX_HEADS = 8
IDX_D = 64
TOPK = 256
QBLK = 128
N_GRP = 4
EXP_PER_GRP = 8
N_EXP = N_GRP * EXP_PER_GRP
EXP_FF = 512
DISP_BLK = 128
THETA = 10000.0
LN_EPS_ = 1e-5
GN_EPS_ = 1e-5
ALPHA = (2 * DEPTH_) ** 0.25

LANES = 128
SUBLANES = 8
INT_MIN = -(2 ** 31)
_CODE_NEG_INF = INT_MIN + 0x7FFFFF
_F32_LOWEST = float(np.finfo(np.float32).min)

_MXU_DTYPE = jnp.bfloat16
_VMEM_LIMIT = 56 * 1024 * 1024

ROW_TILES = D_MODEL_ // LANES
assert ROW_TILES == SUBLANES


def _cparams(sem):
    return pltpu.CompilerParams(dimension_semantics=sem, vmem_limit_bytes=_VMEM_LIMIT)


def _dot(a, b):
    return jnp.dot(a, b, preferred_element_type=jnp.float32)


def _dot_nt(a, b):
    return lax.dot_general(a, b, (((1,), (1,)), ((), ())), preferred_element_type=jnp.float32)


def _dot_tn(a, b):
    return lax.dot_general(a, b, (((0,), (0,)), ((), ())), preferred_element_type=jnp.float32)


def _tile_rows(ref, n, s):
    return ref.at[pl.ds(s, n, stride=ROW_TILES), :]


def _load_tiled(ref, n):
    return jnp.concatenate([_tile_rows(ref, n, s)[...] for s in range(ROW_TILES)], axis=1)


def _store_tiled(ref, val):
    n = val.shape[0]
    for s in range(ROW_TILES):
        _tile_rows(ref, n, s)[...] = val[:, s * LANES:(s + 1) * LANES]


def _swap_halves(a, d):
    tn = a.shape[1]
    if d == 2 * LANES:
        parts = []
        for c in range(0, tn, d):
            parts += [a[:, c + LANES:c + d], a[:, c:c + LANES]]
        return jnp.concatenate(parts, axis=1)
    assert d == LANES
    return jnp.concatenate(
        [pltpu.roll(a[:, c:c + LANES], LANES // 2, axis=1) for c in range(0, tn, LANES)], axis=1)


def _proj_kernel(x_ref, w_ref, *rest, rot):
    if rot:
        c_ref, s_ref, o_ref, w_s = rest
    else:
        o_ref, w_s = rest

    @pl.when((pl.program_id(1) == 0) & (pl.program_id(2) == 0))
    def _():
        w_s[...] = w_ref[...].astype(w_s.dtype)

    acc = _dot(x_ref[...], w_s[...])
    if rot:
        acc = acc * c_ref[...] + _swap_halves(acc, rot) * s_ref[...]
    o_ref[...] = acc.astype(o_ref.dtype)


def _proj(x, w, batch, seq, *, layer=None, col0=0, n_cols=None, rot=0, tables=None,
          out_dtype=None, tm=1024, tn=1024):
    T, K = x.shape
    n_cols = n_cols or w.shape[-1]
    tn = min(tn, n_cols)
    tm = min(tm, seq)
    n_seq = seq // tm
    assert col0 % tn == 0 and n_cols % tn == 0 and seq % tm == 0
    c0 = col0 // tn
    row = lambda j, p, b: (b * n_seq + p, 0)
    if layer is None:
        w_spec = pl.BlockSpec((K, tn), lambda j, p, b: (0, c0 + j))
    else:
        w_spec = pl.BlockSpec((None, K, tn), lambda j, p, b: (layer, 0, c0 + j))
    in_specs = [pl.BlockSpec((tm, K), row), w_spec]
    args = [x, w]
    if rot:
        in_specs += [pl.BlockSpec((tm, tn), lambda j, p, b: (p, 0))] * 2
        args += list(tables)
    return pl.pallas_call(
        functools.partial(_proj_kernel, rot=rot),
        grid=(n_cols // tn, n_seq, batch),
        in_specs=in_specs,
        out_specs=pl.BlockSpec((tm, tn), lambda j, p, b: (b * n_seq + p, j)),
        out_shape=jax.ShapeDtypeStruct((T, n_cols), out_dtype or x.dtype),
        scratch_shapes=[pltpu.VMEM((K, tn), _MXU_DTYPE)],
        compiler_params=_cparams(("parallel", "arbitrary", "arbitrary")),
        name="proj_rot%d" % rot,
    )(*args)


def _rot_tables(seq, d, width, layout):
    inv = 1.0 / (THETA ** (np.arange(0, d, 2, dtype=np.float64) / d))
    ang = np.arange(seq, dtype=np.float64)[:, None] * inv[None, :]
    cos, sin = np.cos(ang), np.sin(ang)
    if layout == "half":
        c = np.concatenate([cos, cos], axis=1)
        s = np.concatenate([-sin, sin], axis=1)
    else:
        z = np.zeros_like(cos)
        c = np.concatenate([cos, z, cos, z], axis=1)
        s = np.concatenate([-sin, z, sin, z], axis=1)
    reps = width // c.shape[1]
    return np.tile(c, (1, reps)).astype(np.float32), np.tile(s, (1, reps)).astype(np.float32)


def _ret_kernel(qk_ref, v_ref, g_ref, dec_ref, xi_ref, zeta_ref, o_ref, r_ref, *, gamma_c):
    H, DK, DV = RET_HEADS, RET_DK, RET_DV

    @pl.when(pl.program_id(1) == 0)
    def _():
        r_ref[...] = jnp.zeros_like(r_ref)

    for h in range(H):
        q = qk_ref[:, h * DK:(h + 1) * DK]
        k = qk_ref[:, (H + h) * DK:(H + h + 1) * DK]
        v = v_ref[:, h * DV:(h + 1) * DV]
        s = _dot_nt(q, k) * dec_ref[h]
        intra = _dot(s.astype(_MXU_DTYPE), v)
        r_old = r_ref[h]
        cross = _dot(q, r_old.astype(_MXU_DTYPE))
        xi = jnp.concatenate([xi_ref[h]] * (DV // LANES), axis=1)
        zeta = jnp.concatenate([zeta_ref[h]] * (DV // LANES), axis=1)
        y = intra + cross * xi
        vz = (v.astype(jnp.float32) * zeta).astype(_MXU_DTYPE)
        r_ref[h] = gamma_c[h] * r_old + _dot_tn(k, vz)
        mu = jnp.mean(y, axis=-1, keepdims=True)
        yc = y - mu
        var = jnp.mean(yc * yc, axis=-1, keepdims=True)
        yn = yc * lax.rsqrt(var + GN_EPS_)
        g = g_ref[:, h * DV:(h + 1) * DV].astype(jnp.float32)
        o_ref[:, h * DV:(h + 1) * DV] = (yn * (g * jax.nn.sigmoid(g))).astype(o_ref.dtype)


def _retention(qk, vg, batch, seq):
    H, DK, DV, C = RET_HEADS, RET_DK, RET_DV, RET_C
    n_chunks = seq // C
    lg = np.log(1.0 - 2.0 ** (-5.0 - np.arange(H, dtype=np.float64)))
    pos = np.arange(C, dtype=np.float64)
    diff = pos[:, None] - pos[None, :]
    dec = np.where(diff >= 0, np.exp(np.maximum(diff, 0.0)[None] * lg[:, None, None]), 0.0)
    dec = (dec * DK ** -0.5).astype(np.float32)
    ones = np.ones((1, 1, LANES))
    xi = (np.exp((pos[None, :] + 1.0) * lg[:, None])[:, :, None] * ones).astype(np.float32)
    zeta = (np.exp((C - 1.0 - pos)[None, :] * lg[:, None])[:, :, None] * ones
            * DK ** -0.5).astype(np.float32)
    gamma_c = tuple(float(g) for g in np.exp(C * lg))
    T = qk.shape[0]
    const3 = lambda b, n: (0, 0, 0)
    row = lambda b, n: (b * n_chunks + n, 0)
    return pl.pallas_call(
        functools.partial(_ret_kernel, gamma_c=gamma_c),
        grid=(batch, n_chunks),
        in_specs=[pl.BlockSpec((C, 2 * H * DK), row),
                  pl.BlockSpec((C, H * DV), row),
                  pl.BlockSpec((C, H * DV), lambda b, n: (b * n_chunks + n, 1)),
                  pl.BlockSpec((H, C, C), const3),
                  pl.BlockSpec((H, C, LANES), const3),
                  pl.BlockSpec((H, C, LANES), const3)],
        out_specs=pl.BlockSpec((C, H * DV), row),
        out_shape=jax.ShapeDtypeStruct((T, H * DV), qk.dtype),
        scratch_shapes=[pltpu.VMEM((H, DK, DV), jnp.float32)],
        compiler_params=_cparams(("parallel", "arbitrary")),
        name="retention",
    )(qk, vg, vg, jnp.asarray(dec), jnp.asarray(xi), jnp.asarray(zeta))


_COUNT_CHAINS = 8
_DSA_CLASSES = 8


def _col_count(mask):
    n, qb = mask.shape
    part = jnp.sum(mask.astype(jnp.int32).reshape(_COUNT_CHAINS, n // _COUNT_CHAINS, qb), axis=1)
    return jnp.sum(part, axis=0, keepdims=True)


def _dsa_block(sv, j, aq_ref, iq_ref, eq_ref, o_ref, k_s, ik_s, vt_s, score_s, bias_s, topk):
    QB = aq_ref.shape[0]
    w_t = eq_ref[:, 3 * LANES:4 * LANES].T * ((IDX_D ** -0.5) * (IDX_HEADS ** -0.5))
    score = jnp.zeros((sv, QB), jnp.float32)
    for h in range(IDX_HEADS):
        rel = jnp.maximum(_dot_nt(ik_s[0:sv, :], iq_ref[:, h * LANES:(h + 1) * LANES]), 0.0)
        score = score + rel * w_t[h:h + 1, :]
    key_pos = lax.broadcasted_iota(jnp.int32, (sv, QB), 0)
    q_pos = j * QB + lax.broadcasted_iota(jnp.int32, (sv, QB), 1)
    score_s[0:sv, :] = jnp.where(key_pos <= q_pos, score, -jnp.inf)

    def count_ge(c):
        return _col_count(score_s[0:sv, :] >= c)

    def float_of(code):
        return pltpu.bitcast(jnp.where(code < 0, code ^ 0x7FFFFFFF, code), jnp.float32)

    code = jnp.where(count_ge(jnp.zeros((1, QB), jnp.float32)) >= topk, 0, INT_MIN)

    def bit_step(i, code):
        cand = code | lax.shift_left(jnp.int32(1), 30 - i)
        ok = (count_ge(float_of(cand)) >= topk) | (cand <= _CODE_NEG_INF)
        return jnp.where(ok, cand, code)

    code = lax.fori_loop(0, 31, bit_step, code)
    thr = float_of(code)
    thr_lo = jnp.maximum(thr, _F32_LOWEST)
    has_tie = jnp.max(jnp.where(code > _CODE_NEG_INF, count_ge(thr) - topk, 0)) > 0

    @pl.when(jnp.logical_not(has_tie))
    def _():
        bias_s[0:sv, :] = jnp.where(score_s[0:sv, :] >= thr_lo, 0.0, -jnp.inf)

    @pl.when(has_tie)
    def _():
        need_f = (topk - _col_count(score_s[0:sv, :] > thr)).astype(jnp.float32)
        r_i = lax.broadcasted_iota(jnp.int32, (LANES, LANES), 0)
        c_i = lax.broadcasted_iota(jnp.int32, (LANES, LANES), 1)
        lower = (r_i > c_i).astype(_MXU_DTYPE)
        carry = jnp.zeros((1, QB), jnp.float32)
        for c in range(sv // LANES):
            sl = slice(c * LANES, (c + 1) * LANES)
            sc = score_s[sl, :]
            eq_c = sc == thr
            eq_f = eq_c.astype(jnp.float32)
            before = _dot(lower, eq_f.astype(_MXU_DTYPE)) + carry
            take = ((sc > thr) | (eq_c & (before < need_f))) & (sc >= _F32_LOWEST)
            bias_s[sl, :] = jnp.where(take, 0.0, -jnp.inf)
            carry = carry + jnp.sum(eq_f, axis=0, keepdims=True)

    c_exp = (ATT_HD ** -0.5) * float(np.log2(np.e))
    for h in range(ATT_HEADS):
        s = _dot_nt(k_s[0:sv, :], aq_ref[:, h * ATT_HD:(h + 1) * ATT_HD]) + bias_s[0:sv, :]
        m = jnp.max(s, axis=0, keepdims=True)
        p = jnp.exp2((s - m) * c_exp)
        l = jnp.sum(p, axis=0, keepdims=True)
        o_t = _dot(vt_s[:, 0:sv], p.astype(_MXU_DTYPE)) / l
        o_ref[:, h * ATT_HD:(h + 1) * ATT_HD] = o_t.T.astype(o_ref.dtype)


def _dsa_kernel(aq_ref, iq_ref, eq_ref, ek_ref, o_ref, k_s, ik_s, vt_s, score_s, bias_s, *, topk):
    S = ek_ref.shape[0]
    QB = aq_ref.shape[0]
    j = pl.program_id(1)

    @pl.when(j == 0)
    def _():
        k_s[...] = ek_ref[:, 0:LANES].astype(k_s.dtype)
        vt_s[...] = ek_ref[:, LANES:2 * LANES].T.astype(vt_s.dtype)
        ik_s[...] = ek_ref[:, 2 * LANES:3 * LANES].astype(ik_s.dtype)

    n_blk = S // QB
    n_cls = min(_DSA_CLASSES, n_blk)
    per_class = n_blk // n_cls
    for c in range(n_cls):
        @pl.when((j >= c * per_class) & (j < (c + 1) * per_class))
        def _(c=c):
            _dsa_block((c + 1) * per_class * QB, j, aq_ref, iq_ref, eq_ref, o_ref,
                       k_s, ik_s, vt_s, score_s, bias_s, topk)


def _dsa(aq, iq, e, batch, seq):
    T = aq.shape[0]
    nb = seq // QBLK
    topk = min(TOPK, seq // 4)
    qrow = lambda b, j: (b * nb + j, 0)
    return pl.pallas_call(
        functools.partial(_dsa_kernel, topk=topk),
        grid=(batch, nb),
        in_specs=[pl.BlockSpec((QBLK, ATT_HEADS * ATT_HD), qrow),
                  pl.BlockSpec((QBLK, IDX_HEADS * LANES), qrow),
                  pl.BlockSpec((QBLK, 4 * LANES), qrow),
                  pl.BlockSpec((seq, 4 * LANES), lambda b, j: (b, 0))],
        out_specs=pl.BlockSpec((QBLK, ATT_HEADS * ATT_HD), qrow),
        out_shape=jax.ShapeDtypeStruct((T, ATT_HEADS * ATT_HD), aq.dtype),
        scratch_shapes=[pltpu.VMEM((seq, LANES), aq.dtype),
                        pltpu.VMEM((seq, LANES), aq.dtype),
                        pltpu.VMEM((LANES, seq), aq.dtype),
                        pltpu.VMEM((seq, QBLK), jnp.float32),
                        pltpu.VMEM((seq, QBLK), jnp.float32)],
        compiler_params=_cparams(("parallel", "arbitrary")),
        name="dsa",
    )(aq, iq, e, e)


def _layer_norm(z, g, b):
    mu = jnp.mean(z, axis=-1, keepdims=True)
    zc = z - mu
    var = jnp.mean(zc * zc, axis=-1, keepdims=True)
    return zc * lax.rsqrt(var + LN_EPS_) * g + b


def _split_hi_lo(a):
    hi = a.astype(_MXU_DTYPE)
    lo = (a - hi.astype(jnp.float32)).astype(_MXU_DTYPE)
    return hi, lo


def _merge_kernel(ret_ref, att_ref, gr_ref, ga_ref, x_ref, wr_ref, wd_ref, wo_ref, g_ref, b_ref,
                  rwh_ref, rwl_ref, rb_ref, h_ref, lg_ref):
    a = _dot(ret_ref[...], wr_ref[...])
    d = _dot(att_ref[...], wd_ref[...])
    merged = (jax.nn.sigmoid(gr_ref[...].astype(jnp.float32)) * a
              + jax.nn.sigmoid(ga_ref[...].astype(jnp.float32)) * d)
    mix = _dot(merged.astype(_MXU_DTYPE), wo_ref[...])
    h = _layer_norm(ALPHA * x_ref[...] + mix, g_ref[...], b_ref[...])
    _store_tiled(h_ref, h)
    h_hi, h_lo = _split_hi_lo(h)
    lg_ref[...] = (_dot(h_hi, rwh_ref[...]) + _dot(h_lo, rwh_ref[...]) + _dot(h_hi, rwl_ref[...])
                   + rb_ref[...])


def _merge(ret, att, gates, x, wr, wd, wo, g, b, rw, rb, tm=256):
    T, D = x.shape
    row = lambda i: (i, 0)
    const = lambda i: (0, 0)
    rw_hi, rw_lo = _split_hi_lo(rw)
    return pl.pallas_call(
        _merge_kernel,
        grid=(T // tm,),
        in_specs=[pl.BlockSpec((tm, ret.shape[1]), row),
                  pl.BlockSpec((tm, att.shape[1]), row),
                  pl.BlockSpec((tm, D), lambda i: (i, 0)),
                  pl.BlockSpec((tm, D), lambda i: (i, 1)),
                  pl.BlockSpec((tm, D), row),
                  pl.BlockSpec(wr.shape, const),
                  pl.BlockSpec(wd.shape, const),
                  pl.BlockSpec(wo.shape, const),
                  pl.BlockSpec((1, D), const),
                  pl.BlockSpec((1, D), const),
                  pl.BlockSpec(rw.shape, const),
                  pl.BlockSpec(rw.shape, const),
                  pl.BlockSpec((1, LANES), const)],
        out_specs=[pl.BlockSpec((tm * ROW_TILES, LANES), row), pl.BlockSpec((tm, LANES), row)],
        out_shape=[jax.ShapeDtypeStruct((T * ROW_TILES, LANES), jnp.float32),
                   jax.ShapeDtypeStruct((T, LANES), jnp.float32)],
        compiler_params=_cparams(("parallel",)),
        name="merge_ln_router",
    )(ret, att, gates, gates, x, wr, wd, wo, g, b, rw_hi, rw_lo, rb)


def _route_kernel(lg_ref, o_ref, cnt_ref, carry_s):
    tm = lg_ref.shape[0]

    @pl.when(pl.program_id(0) == 0)
    def _():
        carry_s[...] = jnp.zeros_like(carry_s)

    lg = lg_ref[...]
    lane = lax.broadcasted_iota(jnp.int32, (tm, LANES), 1).astype(jnp.float32)
    neg = -jnp.inf

    def first_max(v):
        m = jnp.max(v, axis=-1, keepdims=True)
        idx = jnp.min(jnp.where(v == m, lane, float(LANES)), axis=-1, keepdims=True)
        return m, idx

    gl = jnp.where(lane < N_GRP, lg, neg)
    gmax, gsel = first_max(gl)
    p_grp = 1.0 / jnp.sum(jnp.exp(gl - gmax), axis=-1, keepdims=True)
    lo = N_GRP + gsel * EXP_PER_GRP
    el = jnp.where((lane >= lo) & (lane < lo + EXP_PER_GRP), lg, neg)
    top1, i1 = first_max(el)
    top2, i2 = first_max(jnp.where(lane == i1, neg, el))
    z = jnp.exp(top2 - top1)
    g1 = p_grp / (1.0 + z)
    g2 = p_grp * z / (1.0 + z)
    oh1 = lane == i1
    oh2 = lane == i2
    oh = (oh1 | oh2).astype(jnp.float32)
    r_i = lax.broadcasted_iota(jnp.int32, (tm, tm), 0)
    c_i = lax.broadcasted_iota(jnp.int32, (tm, tm), 1)
    lower = (r_i > c_i).astype(_MXU_DTYPE)
    before = _dot(lower, oh.astype(_MXU_DTYPE)) + carry_s[0:1, :]
    rank1 = jnp.sum(jnp.where(oh1, before, 0.0), axis=-1, keepdims=True)
    rank2 = jnp.sum(jnp.where(oh2, before, 0.0), axis=-1, keepdims=True)
    new_carry = carry_s[0:1, :] + jnp.sum(oh, axis=0, keepdims=True)
    carry_s[...] = jnp.broadcast_to(new_carry, carry_s.shape)
    cnt_ref[...] = carry_s[...]
    e1 = i1 - N_GRP
    e2 = i2 - N_GRP
    out = jnp.zeros((tm, LANES), jnp.float32)
    for col, val in enumerate((e1, e2, rank1, rank2, g1, g2)):
        out = jnp.where(lane == col, val, out)
    o_ref[...] = out


def _route(logits, tm=256):
    T = logits.shape[0]
    return pl.pallas_call(
        _route_kernel,
        grid=(T // tm,),
        in_specs=[pl.BlockSpec((tm, LANES), lambda i: (i, 0))],
        out_specs=[pl.BlockSpec((tm, LANES), lambda i: (i, 0)),
                   pl.BlockSpec((SUBLANES, LANES), lambda i: (0, 0))],
        out_shape=[jax.ShapeDtypeStruct((T, LANES), jnp.float32),
                   jax.ShapeDtypeStruct((SUBLANES, LANES), jnp.float32)],
        scratch_shapes=[pltpu.VMEM((SUBLANES, LANES), jnp.float32)],
        compiler_params=_cparams(("arbitrary",)),
        name="route",
    )(logits)


_GATHER_UNROLL = 8


def _token_copy(src_hbm, dst_vmem, sem, src_tok, dst_tok):
    src = src_hbm.at[pl.ds(pl.multiple_of(src_tok * ROW_TILES, ROW_TILES), ROW_TILES)]
    dst = dst_vmem.at[pl.ds(pl.multiple_of(dst_tok * ROW_TILES, ROW_TILES), ROW_TILES)]
    return pltpu.make_async_copy(src, dst, sem)


def _start_token_gather(idx_ref, src_hbm, dst_vmem, sem, n, stride=1, offset=0):
    def issue(r, c):
        _token_copy(src_hbm, dst_vmem, sem, idx_ref[0, 0, offset + r * stride], r).start()
        return c

    lax.fori_loop(0, n, issue, 0, unroll=_GATHER_UNROLL)


def _wait_token_gather(src_hbm, dst_vmem, sem, n):
    pltpu.make_async_copy(src_hbm.at[pl.ds(0, n * ROW_TILES)], dst_vmem, sem).wait()


def _ffn_kernel(be_ref, tok_ref, tok_next_ref, h_hbm, wg_ref, wu_ref, wd_ref, y_ref,
                x_buf, wg_s, wu_s, wd_s, sem):
    i = pl.program_id(0)
    n = pl.num_programs(0)
    slot = i % 2

    @pl.when(i == 0)
    def _():
        _start_token_gather(tok_ref, h_hbm, x_buf.at[0], sem.at[0], DISP_BLK)

    @pl.when(i + 1 < n)
    def _():
        _start_token_gather(tok_next_ref, h_hbm, x_buf.at[1 - slot], sem.at[1 - slot], DISP_BLK)

    @pl.when((i == 0) | (be_ref[i] != be_ref[jnp.maximum(i - 1, 0)]))
    def _():
        wg_s[...] = wg_ref[0].astype(wg_s.dtype)
        wu_s[...] = wu_ref[0].astype(wu_s.dtype)
        wd_s[...] = wd_ref[0].astype(wd_s.dtype)

    _wait_token_gather(h_hbm, x_buf.at[slot], sem.at[slot], DISP_BLK)
    xb = _load_tiled(x_buf.at[slot], DISP_BLK).astype(_MXU_DTYPE)
    gate = _dot(xb, wg_s[...])
    up = _dot(xb, wu_s[...])
    act = (gate * jax.nn.sigmoid(gate)) * up
    _store_tiled(y_ref, _dot(act.astype(_MXU_DTYPE), wd_s[...]))


def _ffn(block_expert, tok_pad, h, wg, wu, wd):
    n_blocks = block_expert.shape[0]
    D, F = wg.shape[1], wg.shape[2]
    wmap = lambda i, be: (be[i], 0, 0)
    tok3 = tok_pad.reshape(n_blocks, 1, DISP_BLK)
    blk_rows = DISP_BLK * ROW_TILES
    grid_spec = pltpu.PrefetchScalarGridSpec(
        num_scalar_prefetch=1,
        grid=(n_blocks,),
        in_specs=[pl.BlockSpec((1, 1, DISP_BLK), lambda i, be: (i, 0, 0), memory_space=pltpu.SMEM),
                  pl.BlockSpec((1, 1, DISP_BLK), lambda i, be: (jnp.minimum(i + 1, n_blocks - 1), 0, 0),
                               memory_space=pltpu.SMEM),
                  pl.BlockSpec(memory_space=pl.ANY),
                  pl.BlockSpec((1, D, F), wmap),
                  pl.BlockSpec((1, D, F), wmap),
                  pl.BlockSpec((1, F, D), wmap)],
        out_specs=pl.BlockSpec((blk_rows, LANES), lambda i, be: (i, 0)),
        scratch_shapes=[pltpu.VMEM((2, blk_rows, LANES), jnp.float32),
                        pltpu.VMEM((D, F), _MXU_DTYPE),
                        pltpu.VMEM((D, F), _MXU_DTYPE),
                        pltpu.VMEM((F, D), _MXU_DTYPE),
                        pltpu.SemaphoreType.DMA((2,))],
    )
    return pl.pallas_call(
        _ffn_kernel,
        grid_spec=grid_spec,
        out_shape=jax.ShapeDtypeStruct((n_blocks * blk_rows, LANES), jnp.float32),
        compiler_params=_cparams(("arbitrary",)),
        name="expert_ffn",
    )(block_expert, tok3, tok3, h, wg, wu, wd)


def _combine_kernel(dest_ref, dest_next_ref, y_hbm, r_ref, h_ref, g_ref, b_ref, o_ref, ob_ref,
                    y_buf, sem):
    tm = o_ref.shape[0]
    i = pl.program_id(0)
    n = pl.num_programs(0)
    slot = i % 2

    def start(idx_ref, s):
        for c in range(2):
            _start_token_gather(idx_ref, y_hbm, y_buf.at[s, c], sem.at[s, c], tm, stride=2, offset=c)

    @pl.when(i == 0)
    def _():
        start(dest_ref, 0)

    @pl.when(i + 1 < n)
    def _():
        start(dest_next_ref, 1 - slot)

    for c in range(2):
        _wait_token_gather(y_hbm, y_buf.at[slot, c], sem.at[slot, c], tm)
    r = r_ref[...]
    g0 = r[:, 4:5]
    g1 = r[:, 5:6]
    z = jnp.concatenate(
        [ALPHA * _tile_rows(h_ref, tm, s)[...]
         + _tile_rows(y_buf.at[slot, 0], tm, s)[...] * g0
         + _tile_rows(y_buf.at[slot, 1], tm, s)[...] * g1 for s in range(ROW_TILES)], axis=1)
    out = _layer_norm(z, g_ref[...], b_ref[...])
    o_ref[...] = out
    ob_ref[...] = out.astype(ob_ref.dtype)


def _combine(dest, y_pad, route, h, g, b, tm=128):
    T = route.shape[0]
    D = g.shape[1]
    n = T // tm
    row = lambda i: (i, 0)
    const = lambda i: (0, 0)
    dest3 = dest.reshape(n, 1, 2 * tm)
    return pl.pallas_call(
        _combine_kernel,
        grid=(n,),
        in_specs=[pl.BlockSpec((1, 1, 2 * tm), lambda i: (i, 0, 0), memory_space=pltpu.SMEM),
                  pl.BlockSpec((1, 1, 2 * tm), lambda i: (jnp.minimum(i + 1, n - 1), 0, 0),
                               memory_space=pltpu.SMEM),
                  pl.BlockSpec(memory_space=pl.ANY),
                  pl.BlockSpec((tm, LANES), row),
                  pl.BlockSpec((tm * ROW_TILES, LANES), row),
                  pl.BlockSpec((1, D), const),
                  pl.BlockSpec((1, D), const)],
        out_specs=[pl.BlockSpec((tm, D), row), pl.BlockSpec((tm, D), row)],
        out_shape=[jax.ShapeDtypeStruct((T, D), jnp.float32),
                   jax.ShapeDtypeStruct((T, D), _MXU_DTYPE)],
        scratch_shapes=[pltpu.VMEM((2, 2, tm * ROW_TILES, LANES), jnp.float32),
                        pltpu.SemaphoreType.DMA((2, 2))],
        compiler_params=_cparams(("arbitrary",)),
        name="combine_ln",
    )(dest3, dest3, y_pad, route, h, g, b)


def _spread64(w):
    K, n = w.shape[0], w.shape[1] // IDX_D
    w = w.reshape(K, n, 2, IDX_D // 2)
    w = jnp.pad(w, ((0, 0), (0, 0), (0, 0), (0, IDX_D // 2)))
    return w.reshape(K, n * LANES)


_C_RQ = 0
_C_RV = 2 * RET_HEADS * RET_DK
_C_AQ = _C_RV + 2 * RET_HEADS * RET_DV
_C_AK = _C_AQ + ATT_HEADS * ATT_HD
_C_IQ = _C_AK + 2 * ATT_HD
_C_IK = _C_IQ + IDX_HEADS * IDX_D
_C_IW = _C_IK + IDX_D
_C_GR = _C_IW + IDX_HEADS
_C_END = _C_GR + 2 * D_MODEL_


def _moe_plan(route, counts):
    T = route.shape[0]
    n_assign = T * 2
    n_blocks = -(-n_assign // DISP_BLK) + N_EXP
    cnt = counts[0, N_GRP:N_GRP + N_EXP].astype(jnp.int32)
    padded = (cnt + DISP_BLK - 1) // DISP_BLK * DISP_BLK
    pad_ends = jnp.cumsum(padded)
    pad_starts = pad_ends - padded
    expert = route[:, 0:2].astype(jnp.int32)
    rank = route[:, 2:4].astype(jnp.int32)
    dest = (pad_starts[expert] + rank).reshape(n_assign)
    tok = jnp.repeat(jnp.arange(T, dtype=jnp.int32), 2)
    tok_pad = jnp.zeros((n_blocks * DISP_BLK,), jnp.int32).at[dest].set(tok)
    block_start = jnp.arange(n_blocks, dtype=jnp.int32) * DISP_BLK
    block_expert = jnp.minimum(
        jnp.sum((pad_ends[None, :] <= block_start[:, None]).astype(jnp.int32), axis=1), N_EXP - 1)
    return dest, tok_pad, block_expert


def kernel(x, w_in, w_ret_o, w_dsa_o, w_out, ln1_g, ln1_b, router_grp_w, router_grp_b,
           router_exp_w, router_exp_b, w_gate, w_up, w_down, ln2_g, ln2_b):
    B, S, D = x.shape
    T = B * S
    cdt = _MXU_DTYPE
    assert w_in.shape[-1] == _C_END
    t_ret = [jnp.asarray(t) for t in _rot_tables(S, RET_DK, 4 * RET_DK, "half")]
    t_att = [jnp.asarray(t) for t in _rot_tables(S, ATT_HD, 8 * ATT_HD, "half")]
    t_idx = [jnp.asarray(t) for t in _rot_tables(S, IDX_D, 8 * LANES, "spread")]
    ident = (np.ones((S, LANES), np.float32), np.zeros((S, LANES), np.float32))
    t_e = [jnp.asarray(np.concatenate([ta[:, :LANES], i, ti[:, :LANES], i], axis=1))
           for ta, ti, i in zip(_rot_tables(S, ATT_HD, LANES, "half"),
                                _rot_tables(S, IDX_D, LANES, "spread"), ident)]
    xf = x.reshape(T, D)
    xb = xf.astype(cdt)
    for l in range(DEPTH_):
        wl = w_in[l]
        w_iq = _spread64(wl[:, _C_IQ:_C_IK])
        w_e = jnp.concatenate([wl[:, _C_AK:_C_IQ], _spread64(wl[:, _C_IK:_C_IW]),
                               jnp.pad(wl[:, _C_IW:_C_GR], ((0, 0), (0, LANES - IDX_HEADS)))], axis=1)
        w_gates = wl[:, _C_GR:_C_END]
        qk = _proj(xb, w_in, B, S, layer=l, col0=_C_RQ, n_cols=_C_RV - _C_RQ, rot=RET_DK,
                   tables=t_ret, tm=512)
        vg = _proj(xb, w_in, B, S, layer=l, col0=_C_RV, n_cols=_C_AQ - _C_RV)
        aq = _proj(xb, w_in, B, S, layer=l, col0=_C_AQ, n_cols=_C_AK - _C_AQ, rot=ATT_HD,
                   tables=t_att, tm=512)
        iq = _proj(xb, w_iq, B, S, rot=LANES, tables=t_idx, tm=512)
        e = _proj(xb, w_e, B, S, rot=LANES, tables=t_e, out_dtype=jnp.float32, tm=512)
        gates = _proj(xb, w_gates, B, S)
        ret = _retention(qk, vg, B, S)
        att = _dsa(aq, iq, e, B, S)
        rw = jnp.pad(jnp.concatenate([router_grp_w[l], router_exp_w[l]], axis=1),
                     ((0, 0), (0, LANES - N_GRP - N_EXP)))
        rb = jnp.pad(jnp.concatenate([router_grp_b[l], router_exp_b[l]]),
                     (0, LANES - N_GRP - N_EXP)).reshape(1, LANES)
        h, logits = _merge(ret, att, gates, xf, w_ret_o[l].astype(cdt), w_dsa_o[l].astype(cdt),
                           w_out[l].astype(cdt), ln1_g[l].reshape(1, D), ln1_b[l].reshape(1, D),
                           rw, rb)
        route, counts = _route(logits)
        dest, tok_pad, block_expert = _moe_plan(route, counts)
        y_pad = _ffn(block_expert, tok_pad, h, w_gate[l], w_up[l], w_down[l])
        xf, xb = _combine(dest, y_pad, route, h, ln2_g[l].reshape(1, D), ln2_b[l].reshape(1, D))
    return xf.reshape(B, S, D)
```

```python
import functools

import numpy as np
import jax
import jax.numpy as jnp
from jax import lax
from jax.experimental import pallas as pl
from jax.experimental.pallas import tpu as pltpu

D_MODEL_ = 1024
DEPTH_ = 2
RET_HEADS = 4
RET_DK = 256
RET_DV = 512
RET_C = 128
ATT_HEADS = 8
ATT_HD = 128
IDX_HEADS = 8
IDX_D = 64
TOPK = 256
QBLK = 128
N_GRP = 4
EXP_PER_GRP = 8
N_EXP = N_GRP * EXP_PER_GRP
EXP_FF = 512
DISP_BLK = 128
THETA = 10000.0
LN_EPS_ = 1e-5
GN_EPS_ = 1e-5
ALPHA = (2 * DEPTH_) ** 0.25

LANES = 128
SUBLANES = 8
INT_MIN = -(2 ** 31)
_CODE_NEG_INF = INT_MIN + 0x7FFFFF
_F32_LOWEST = float(np.finfo(np.float32).min)

_MXU_DTYPE = jnp.bfloat16
_VMEM_LIMIT = 56 * 1024 * 1024

ROW_TILES = D_MODEL_ // LANES
assert ROW_TILES == SUBLANES


def _cparams(sem):
    return pltpu.CompilerParams(dimension_semantics=sem, vmem_limit_bytes=_VMEM_LIMIT)


def _dot(a, b):
    return jnp.dot(a, b, preferred_element_type=jnp.float32)


def _dot_nt(a, b):
    return lax.dot_general(a, b, (((1,), (1,)), ((), ())), preferred_element_type=jnp.float32)


def _dot_tn(a, b):
    return lax.dot_general(a, b, (((0,), (0,)), ((), ())), preferred_element_type=jnp.float32)


def _tile_rows(ref, n, s):
    return ref.at[pl.ds(s, n, stride=ROW_TILES), :]


def _load_tiled(ref, n):
    return jnp.concatenate([_tile_rows(ref, n, s)[...] for s in range(ROW_TILES)], axis=1)


def _store_tiled(ref, val):
    n = val.shape[0]
    for s in range(ROW_TILES):
        _tile_rows(ref, n, s)[...] = val[:, s * LANES:(s + 1) * LANES]


def _swap_halves(a, d):
    tn = a.shape[1]
    if d == 2 * LANES:
        parts = []
        for c in range(0, tn, d):
            parts += [a[:, c + LANES:c + d], a[:, c:c + LANES]]
        return jnp.concatenate(parts, axis=1)
    assert d == LANES
    return jnp.concatenate(
        [pltpu.roll(a[:, c:c + LANES], LANES // 2, axis=1) for c in range(0, tn, LANES)], axis=1)


def _proj_kernel(x_ref, w_ref, *rest, rot):
    if rot:
        c_ref, s_ref, o_ref, w_s = rest
    else:
        o_ref, w_s = rest

    @pl.when((pl.program_id(1) == 0) & (pl.program_id(2) == 0))
    def _():
        w_s[...] = w_ref[...].astype(w_s.dtype)

    acc = _dot_nt(x_ref[...], w_s[...])
    if rot:
        acc = acc * c_ref[...] + _swap_halves(acc, rot) * s_ref[...]
    o_ref[...] = acc.astype(o_ref.dtype)


def _proj(x, wt, batch, seq, *, layer=None, col0=0, n_cols=None, rot=0, tables=None,
          out_dtype=None, tm=1024, tn=1024):
    T, K = x.shape
    n_cols = n_cols or wt.shape[-2]
    tn = min(tn, n_cols)
    tm = min(tm, seq)
    n_seq = seq // tm
    assert col0 % tn == 0 and n_cols % tn == 0 and seq % tm == 0
    c0 = col0 // tn
    row = lambda j, p, b: (b * n_seq + p, 0)
    if layer is None:
        w_spec = pl.BlockSpec((tn, K), lambda j, p, b: (c0 + j, 0))
    else:
        w_spec = pl.BlockSpec((None, tn, K), lambda j, p, b: (layer, c0 + j, 0))
    in_specs = [pl.BlockSpec((tm, K), row), w_spec]
    args = [x, wt]
    if rot:
        in_specs += [pl.BlockSpec((tm, tn), lambda j, p, b: (p, 0))] * 2
        args += list(tables)
    return pl.pallas_call(
        functools.partial(_proj_kernel, rot=rot),
        grid=(n_cols // tn, n_seq, batch),
        in_specs=in_specs,
        out_specs=pl.BlockSpec((tm, tn), lambda j, p, b: (b * n_seq + p, j)),
        out_shape=jax.ShapeDtypeStruct((T, n_cols), out_dtype or x.dtype),
        scratch_shapes=[pltpu.VMEM((tn, K), _MXU_DTYPE)],
        compiler_params=_cparams(("parallel", "arbitrary", "arbitrary")),
        name="proj_rot%d" % rot,
    )(*args)


def _rot_tables(seq, d, width, layout):
    inv = 1.0 / (THETA ** (np.arange(0, d, 2, dtype=np.float64) / d))
    ang = np.arange(seq, dtype=np.float64)[:, None] * inv[None, :]
    cos, sin = np.cos(ang), np.sin(ang)
    if layout == "half":
        c = np.concatenate([cos, cos], axis=1)
        s = np.concatenate([-sin, sin], axis=1)
    else:
        z = np.zeros_like(cos)
        c = np.concatenate([cos, z, cos, z], axis=1)
        s = np.concatenate([-sin, z, sin, z], axis=1)
    reps = width // c.shape[1]
    return np.tile(c, (1, reps)).astype(np.float32), np.tile(s, (1, reps)).astype(np.float32)


def _ret_kernel(qk_ref, v_ref, g_ref, dec_ref, xi_ref, zeta_ref, o_ref, r_ref, *, gamma_c):
    H, DK, DV = RET_HEADS, RET_DK, RET_DV

    @pl.when(pl.program_id(1) == 0)
    def _():
        r_ref[...] = jnp.zeros_like(r_ref)

    for h in range(H):
        q = qk_ref[:, h * DK:(h + 1) * DK]
        k = qk_ref[:, (H + h) * DK:(H + h + 1) * DK]
        v = v_ref[:, h * DV:(h + 1) * DV]
        s = _dot_nt(q, k) * dec_ref[h]
        intra = _dot(s.astype(_MXU_DTYPE), v)
        r_old = r_ref[h]
        cross = _dot(q, r_old.astype(_MXU_DTYPE))
        xi = jnp.concatenate([xi_ref[h]] * (DV // LANES), axis=1)
        zeta = jnp.concatenate([zeta_ref[h]] * (DV // LANES), axis=1)
        y = intra + cross * xi
        vz = (v.astype(jnp.float32) * zeta).astype(_MXU_DTYPE)
        r_ref[h] = gamma_c[h] * r_old + _dot_tn(k, vz)
        mu = jnp.mean(y, axis=-1, keepdims=True)
        yc = y - mu
        var = jnp.mean(yc * yc, axis=-1, keepdims=True)
        yn = yc * lax.rsqrt(var + GN_EPS_)
        g = g_ref[:, h * DV:(h + 1) * DV].astype(jnp.float32)
        o_ref[:, h * DV:(h + 1) * DV] = (yn * (g * jax.nn.sigmoid(g))).astype(o_ref.dtype)


def _retention(qk, vg, batch, seq):
    H, DK, DV, C = RET_HEADS, RET_DK, RET_DV, RET_C
    n_chunks = seq // C
    lg = np.log(1.0 - 2.0 ** (-5.0 - np.arange(H, dtype=np.float64)))
    pos = np.arange(C, dtype=np.float64)
    diff = pos[:, None] - pos[None, :]
    dec = np.where(diff >= 0, np.exp(np.maximum(diff, 0.0)[None] * lg[:, None, None]), 0.0)
    dec = (dec * DK ** -0.5).astype(np.float32)
    ones = np.ones((1, 1, LANES))
    xi = (np.exp((pos[None, :] + 1.0) * lg[:, None])[:, :, None] * ones).astype(np.float32)
    zeta = (np.exp((C - 1.0 - pos)[None, :] * lg[:, None])[:, :, None] * ones
            * DK ** -0.5).astype(np.float32)
    gamma_c = tuple(float(g) for g in np.exp(C * lg))
    T = qk.shape[0]
    const3 = lambda b, n: (0, 0, 0)
    row = lambda b, n: (b * n_chunks + n, 0)
    return pl.pallas_call(
        functools.partial(_ret_kernel, gamma_c=gamma_c),
        grid=(batch, n_chunks),
        in_specs=[pl.BlockSpec((C, 2 * H * DK), row),
                  pl.BlockSpec((C, H * DV), row),
                  pl.BlockSpec((C, H * DV), lambda b, n: (b * n_chunks + n, 1)),
                  pl.BlockSpec((H, C, C), const3),
                  pl.BlockSpec((H, C, LANES), const3),
                  pl.BlockSpec((H, C, LANES), const3)],
        out_specs=pl.BlockSpec((C, H * DV), row),
        out_shape=jax.ShapeDtypeStruct((T, H * DV), qk.dtype),
        scratch_shapes=[pltpu.VMEM((H, DK, DV), jnp.float32)],
        compiler_params=_cparams(("parallel", "arbitrary")),
        name="retention",
    )(qk, vg, vg, jnp.asarray(dec), jnp.asarray(xi), jnp.asarray(zeta))


_COUNT_CHAINS = 8
_DSA_CLASSES = 8


def _col_count(mask):
    n, qb = mask.shape
    part = jnp.sum(mask.astype(jnp.int32).reshape(_COUNT_CHAINS, n // _COUNT_CHAINS, qb), axis=1)
    return jnp.sum(part, axis=0, keepdims=True)


def _dsa_block(sv, j, aq_ref, iq_ref, eq_ref, o_ref, k_s, ik_s, vt_s, score_s, bias_s, topk):
    QB = aq_ref.shape[0]
    w_t = eq_ref[:, 3 * LANES:4 * LANES].T * ((IDX_D ** -0.5) * (IDX_HEADS ** -0.5))
    score = jnp.zeros((sv, QB), jnp.float32)
    for h in range(IDX_HEADS):
        rel = jnp.maximum(_dot_nt(ik_s[0:sv, :], iq_ref[:, h * LANES:(h + 1) * LANES]), 0.0)
        score = score + rel * w_t[h:h + 1, :]
    key_pos = lax.broadcasted_iota(jnp.int32, (sv, QB), 0)
    q_pos = j * QB + lax.broadcasted_iota(jnp.int32, (sv, QB), 1)
    score_s[0:sv, :] = jnp.where(key_pos <= q_pos, score, -jnp.inf)

    def count_ge(c):
        return _col_count(score_s[0:sv, :] >= c)

    def float_of(code):
        return pltpu.bitcast(jnp.where(code < 0, code ^ 0x7FFFFFFF, code), jnp.float32)

    code = jnp.where(count_ge(jnp.zeros((1, QB), jnp.float32)) >= topk, 0, INT_MIN)

    def bit_step(i, code):
        cand = code | lax.shift_left(jnp.int32(1), 30 - i)
        ok = (count_ge(float_of(cand)) >= topk) | (cand <= _CODE_NEG_INF)
        return jnp.where(ok, cand, code)

    code = lax.fori_loop(0, 31, bit_step, code)
    thr = float_of(code)
    thr_lo = jnp.maximum(thr, _F32_LOWEST)
    has_tie = jnp.max(jnp.where(code > _CODE_NEG_INF, count_ge(thr) - topk, 0)) > 0

    @pl.when(jnp.logical_not(has_tie))
    def _():
        bias_s[0:sv, :] = jnp.where(score_s[0:sv, :] >= thr_lo, 0.0, -jnp.inf)

    @pl.when(has_tie)
    def _():
        need_f = (topk - _col_count(score_s[0:sv, :] > thr)).astype(jnp.float32)
        r_i = lax.broadcasted_iota(jnp.int32, (LANES, LANES), 0)
        c_i = lax.broadcasted_iota(jnp.int32, (LANES, LANES), 1)
        lower = (r_i > c_i).astype(_MXU_DTYPE)
        carry = jnp.zeros((1, QB), jnp.float32)
        for c in range(sv // LANES):
            sl = slice(c * LANES, (c + 1) * LANES)
            sc = score_s[sl, :]
            eq_c = sc == thr
            eq_f = eq_c.astype(jnp.float32)
            before = _dot(lower, eq_f.astype(_MXU_DTYPE)) + carry
            take = ((sc > thr) | (eq_c & (before < need_f))) & (sc >= _F32_LOWEST)
            bias_s[sl, :] = jnp.where(take, 0.0, -jnp.inf)
            carry = carry + jnp.sum(eq_f, axis=0, keepdims=True)

    c_exp = (ATT_HD ** -0.5) * float(np.log2(np.e))
    for h in range(ATT_HEADS):
        s = _dot_nt(k_s[0:sv, :], aq_ref[:, h * ATT_HD:(h + 1) * ATT_HD]) + bias_s[0:sv, :]
        m = jnp.max(s, axis=0, keepdims=True)
        p = jnp.exp2((s - m) * c_exp)
        l = jnp.sum(p, axis=0, keepdims=True)
        o_t = _dot(vt_s[:, 0:sv], p.astype(_MXU_DTYPE)) / l
        o_ref[:, h * ATT_HD:(h + 1) * ATT_HD] = o_t.T.astype(o_ref.dtype)


def _dsa_kernel(aq_ref, iq_ref, eq_ref, ek_ref, o_ref, k_s, ik_s, vt_s, score_s, bias_s, *, topk):
    S = ek_ref.shape[0]
    QB = aq_ref.shape[0]
    j = pl.program_id(1)

    @pl.when(j == 0)
    def _():
        k_s[...] = ek_ref[:, 0:LANES].astype(k_s.dtype)
        vt_s[...] = ek_ref[:, LANES:2 * LANES].T.astype(vt_s.dtype)
        ik_s[...] = ek_ref[:, 2 * LANES:3 * LANES].astype(ik_s.dtype)

    n_blk = S // QB
    n_cls = min(_DSA_CLASSES, n_blk)
    per_class = n_blk // n_cls
    for c in range(n_cls):
        @pl.when((j >= c * per_class) & (j < (c + 1) * per_class))
        def _(c=c):
            _dsa_block((c + 1) * per_class * QB, j, aq_ref, iq_ref, eq_ref, o_ref,
                       k_s, ik_s, vt_s, score_s, bias_s, topk)


def _dsa(aq, iq, e, batch, seq):
    T = aq.shape[0]
    nb = seq // QBLK
    topk = min(TOPK, seq // 4)
    qrow = lambda b, j: (b * nb + j, 0)
    return pl.pallas_call(
        functools.partial(_dsa_kernel, topk=topk),
        grid=(batch, nb),
        in_specs=[pl.BlockSpec((QBLK, ATT_HEADS * ATT_HD), qrow),
                  pl.BlockSpec((QBLK, IDX_HEADS * LANES), qrow),
                  pl.BlockSpec((QBLK, 4 * LANES), qrow),
                  pl.BlockSpec((seq, 4 * LANES), lambda b, j: (b, 0))],
        out_specs=pl.BlockSpec((QBLK, ATT_HEADS * ATT_HD), qrow),
        out_shape=jax.ShapeDtypeStruct((T, ATT_HEADS * ATT_HD), aq.dtype),
        scratch_shapes=[pltpu.VMEM((seq, LANES), aq.dtype),
                        pltpu.VMEM((seq, LANES), aq.dtype),
                        pltpu.VMEM((LANES, seq), aq.dtype),
                        pltpu.VMEM((seq, QBLK), jnp.float32),
                        pltpu.VMEM((seq, QBLK), jnp.float32)],
        compiler_params=_cparams(("parallel", "arbitrary")),
        name="dsa",
    )(aq, iq, e, e)


def _layer_norm(z, g, b):
    mu = jnp.mean(z, axis=-1, keepdims=True)
    zc = z - mu
    var = jnp.mean(zc * zc, axis=-1, keepdims=True)
    return zc * lax.rsqrt(var + LN_EPS_) * g + b


def _split_hi_lo(a):
    hi = a.astype(_MXU_DTYPE)
    lo = (a - hi.astype(jnp.float32)).astype(_MXU_DTYPE)
    return hi, lo


def _merge_kernel(ret_ref, att_ref, gr_ref, ga_ref, x_ref, wr_ref, wd_ref, wo_ref, g_ref, b_ref,
                  rwh_ref, rwl_ref, rb_ref, h_ref, lg_ref):
    a = _dot(ret_ref[...], wr_ref[...])
    d = _dot(att_ref[...], wd_ref[...])
    merged = (jax.nn.sigmoid(gr_ref[...].astype(jnp.float32)) * a
              + jax.nn.sigmoid(ga_ref[...].astype(jnp.float32)) * d)
    mix = _dot(merged.astype(_MXU_DTYPE), wo_ref[...])
    h = _layer_norm(ALPHA * x_ref[...] + mix, g_ref[...], b_ref[...])
    _store_tiled(h_ref, h)
    h_hi, h_lo = _split_hi_lo(h)
    lg_ref[...] = (_dot(h_hi, rwh_ref[...]) + _dot(h_lo, rwh_ref[...]) + _dot(h_hi, rwl_ref[...])
                   + rb_ref[...])


def _merge(ret, att, gates, x, wr, wd, wo, g, b, rw, rb, tm=512):
    T, D = x.shape
    row = lambda i: (i, 0)
    const = lambda i: (0, 0)
    rw_hi, rw_lo = _split_hi_lo(rw)
    return pl.pallas_call(
        _merge_kernel,
        grid=(T // tm,),
        in_specs=[pl.BlockSpec((tm, ret.shape[1]), row),
                  pl.BlockSpec((tm, att.shape[1]), row),
                  pl.BlockSpec((tm, D), lambda i: (i, 0)),
                  pl.BlockSpec((tm, D), lambda i: (i, 1)),
                  pl.BlockSpec((tm, D), row),
                  pl.BlockSpec(wr.shape, const),
                  pl.BlockSpec(wd.shape, const),
                  pl.BlockSpec(wo.shape, const),
                  pl.BlockSpec((1, D), const),
                  pl.BlockSpec((1, D), const),
                  pl.BlockSpec(rw.shape, const),
                  pl.BlockSpec(rw.shape, const),
                  pl.BlockSpec((1, LANES), const)],
        out_specs=[pl.BlockSpec((tm * ROW_TILES, LANES), row), pl.BlockSpec((tm, LANES), row)],
        out_shape=[jax.ShapeDtypeStruct((T * ROW_TILES, LANES), jnp.float32),
                   jax.ShapeDtypeStruct((T, LANES), jnp.float32)],
        compiler_params=_cparams(("parallel",)),
        name="merge_ln_router",
    )(ret, att, gates, gates, x, wr, wd, wo, g, b, rw_hi, rw_lo, rb)


def _route_kernel(lg_ref, o_ref, cnt_ref, carry_s):
    tm = lg_ref.shape[0]

    @pl.when(pl.program_id(0) == 0)
    def _():
        carry_s[...] = jnp.zeros_like(carry_s)

    lg = lg_ref[...]
    lane = lax.broadcasted_iota(jnp.int32, (tm, LANES), 1).astype(jnp.float32)
    neg = -jnp.inf

    def first_max(v):
        m = jnp.max(v, axis=-1, keepdims=True)
        idx = jnp.min(jnp.where(v == m, lane, float(LANES)), axis=-1, keepdims=True)
        return m, idx

    gl = jnp.where(lane < N_GRP, lg, neg)
    gmax, gsel = first_max(gl)
    p_grp = 1.0 / jnp.sum(jnp.exp(gl - gmax), axis=-1, keepdims=True)
    lo = N_GRP + gsel * EXP_PER_GRP
    el = jnp.where((lane >= lo) & (lane < lo + EXP_PER_GRP), lg, neg)
    top1, i1 = first_max(el)
    top2, i2 = first_max(jnp.where(lane == i1, neg, el))
    z = jnp.exp(top2 - top1)
    g1 = p_grp / (1.0 + z)
    g2 = p_grp * z / (1.0 + z)
    oh1 = lane == i1
    oh2 = lane == i2
    oh = (oh1 | oh2).astype(jnp.float32)
    r_i = lax.broadcasted_iota(jnp.int32, (tm, tm), 0)
    c_i = lax.broadcasted_iota(jnp.int32, (tm, tm), 1)
    lower = (r_i > c_i).astype(_MXU_DTYPE)
    before = _dot(lower, oh.astype(_MXU_DTYPE)) + carry_s[0:1, :]
    rank1 = jnp.sum(jnp.where(oh1, before, 0.0), axis=-1, keepdims=True)
    rank2 = jnp.sum(jnp.where(oh2, before, 0.0), axis=-1, keepdims=True)
    new_carry = carry_s[0:1, :] + jnp.sum(oh, axis=0, keepdims=True)
    carry_s[...] = jnp.broadcast_to(new_carry, carry_s.shape)
    cnt_ref[...] = carry_s[...]
    e1 = i1 - N_GRP
    e2 = i2 - N_GRP
    out = jnp.zeros((tm, LANES), jnp.float32)
    for col, val in enumerate((e1, e2, rank1, rank2, g1, g2)):
        out = jnp.where(lane == col, val, out)
    o_ref[...] = out


def _route(logits, tm=256):
    T = logits.shape[0]
    return pl.pallas_call(
        _route_kernel,
        grid=(T // tm,),
        in_specs=[pl.BlockSpec((tm, LANES), lambda i: (i, 0))],
        out_specs=[pl.BlockSpec((tm, LANES), lambda i: (i, 0)),
                   pl.BlockSpec((SUBLANES, LANES), lambda i: (0, 0))],
        out_shape=[jax.ShapeDtypeStruct((T, LANES), jnp.float32),
                   jax.ShapeDtypeStruct((SUBLANES, LANES), jnp.float32)],
        scratch_shapes=[pltpu.VMEM((SUBLANES, LANES), jnp.float32)],
        compiler_params=_cparams(("arbitrary",)),
        name="route",
    )(logits)


_GATHER_UNROLL = 8


def _token_copy(src_hbm, dst_vmem, sem, src_tok, dst_tok):
    src = src_hbm.at[pl.ds(pl.multiple_of(src_tok * ROW_TILES, ROW_TILES), ROW_TILES)]
    dst = dst_vmem.at[pl.ds(pl.multiple_of(dst_tok * ROW_TILES, ROW_TILES), ROW_TILES)]
    return pltpu.make_async_copy(src, dst, sem)


def _start_token_gather(idx_ref, src_hbm, dst_vmem, sem, n, stride=1, offset=0, first=0):
    def issue(q, c):
        for prio in range(2):
            r = first + 2 * q + prio
            _token_copy(src_hbm, dst_vmem, sem, idx_ref[0, 0, offset + r * stride], r).start(
                priority=prio)
        return c

    lax.fori_loop(0, n // 2, issue, 0, unroll=_GATHER_UNROLL // 2)


def _wait_token_gather(src_hbm, dst_vmem, sem, n):
    pltpu.make_async_copy(src_hbm.at[pl.ds(0, n * ROW_TILES)], dst_vmem, sem).wait()


def _ffn_kernel(be_ref, tok_ref, tok_next_ref, h_hbm, wg_ref, wu_ref, wd_ref, y_ref,
                x_buf, wg_s, wu_s, wd_s, sem):
    i = pl.program_id(0)
    n = pl.num_programs(0)
    slot = i % 2

    @pl.when(i == 0)
    def _():
        _start_token_gather(tok_ref, h_hbm, x_buf.at[0], sem.at[0], DISP_BLK)

    def prefetch(part):
        @pl.when(i + 1 < n)
        def _():
            _start_token_gather(tok_next_ref, h_hbm, x_buf.at[1 - slot], sem.at[1 - slot],
                                DISP_BLK // 4, first=part * (DISP_BLK // 4))

    @pl.when((i == 0) | (be_ref[i] != be_ref[jnp.maximum(i - 1, 0)]))
    def _():
        wg_s[...] = wg_ref[0].astype(wg_s.dtype)
        wu_s[...] = wu_ref[0].astype(wu_s.dtype)
        wd_s[...] = wd_ref[0].astype(wd_s.dtype)

    _wait_token_gather(h_hbm, x_buf.at[slot], sem.at[slot], DISP_BLK)
    xb = _load_tiled(x_buf.at[slot], DISP_BLK).astype(_MXU_DTYPE)
    prefetch(0)
    gate = _dot(xb, wg_s[...])
    prefetch(1)
    up = _dot(xb, wu_s[...])
    prefetch(2)
    act = (gate * jax.nn.sigmoid(gate)) * up
    y = _dot(act.astype(_MXU_DTYPE), wd_s[...])
    prefetch(3)
    _store_tiled(y_ref, y)


def _ffn(block_expert, tok_pad, h, wg, wu, wd, layer):
    n_blocks = block_expert.shape[0]
    D, F = wg.shape[2], wg.shape[3]
    wmap = lambda i, be: (layer, be[i], 0, 0)
    tok3 = tok_pad.reshape(n_blocks, 1, DISP_BLK)
    blk_rows = DISP_BLK * ROW_TILES
    grid_spec = pltpu.PrefetchScalarGridSpec(
        num_scalar_prefetch=1,
        grid=(n_blocks,),
        in_specs=[pl.BlockSpec((1, 1, DISP_BLK), lambda i, be: (i, 0, 0), memory_space=pltpu.SMEM),
                  pl.BlockSpec((1, 1, DISP_BLK), lambda i, be: (jnp.minimum(i + 1, n_blocks - 1), 0, 0),
                               memory_space=pltpu.SMEM),
                  pl.BlockSpec(memory_space=pl.ANY),
                  pl.BlockSpec((None, 1, D, F), wmap),
                  pl.BlockSpec((None, 1, D, F), wmap),
                  pl.BlockSpec((None, 1, F, D), wmap)],
        out_specs=pl.BlockSpec((blk_rows, LANES), lambda i, be: (i, 0)),
        scratch_shapes=[pltpu.VMEM((2, blk_rows, LANES), jnp.float32),
                        pltpu.VMEM((D, F), _MXU_DTYPE),
                        pltpu.VMEM((D, F), _MXU_DTYPE),
                        pltpu.VMEM((F, D), _MXU_DTYPE),
                        pltpu.SemaphoreType.DMA((2,))],
    )
    return pl.pallas_call(
        _ffn_kernel,
        grid_spec=grid_spec,
        out_shape=jax.ShapeDtypeStruct((n_blocks * blk_rows, LANES), jnp.float32),
        compiler_params=_cparams(("arbitrary",)),
        name="expert_ffn",
    )(block_expert, tok3, tok3, h, wg, wu, wd)


def _combine_kernel(dest_ref, dest_next_ref, y_hbm, r_ref, h_ref, g_ref, b_ref, o_ref, ob_ref,
                    y_buf, sem):
    tm = o_ref.shape[0]
    i = pl.program_id(0)
    n = pl.num_programs(0)
    slot = i % 2

    def start(idx_ref, s):
        for c in range(2):
            _start_token_gather(idx_ref, y_hbm, y_buf.at[s, c], sem.at[s, c], tm, stride=2, offset=c)

    @pl.when(i == 0)
    def _():
        start(dest_ref, 0)

    @pl.when(i + 1 < n)
    def _():
        start(dest_next_ref, 1 - slot)

    for c in range(2):
        _wait_token_gather(y_hbm, y_buf.at[slot, c], sem.at[slot, c], tm)
    r = r_ref[...]
    g0 = r[:, 4:5]
    g1 = r[:, 5:6]
    z = jnp.concatenate(
        [ALPHA * _tile_rows(h_ref, tm, s)[...]
         + _tile_rows(y_buf.at[slot, 0], tm, s)[...] * g0
         + _tile_rows(y_buf.at[slot, 1], tm, s)[...] * g1 for s in range(ROW_TILES)], axis=1)
    out = _layer_norm(z, g_ref[...], b_ref[...])
    o_ref[...] = out
    ob_ref[...] = out.astype(ob_ref.dtype)


def _combine(dest, y_pad, route, h, g, b, tm=128):
    T = route.shape[0]
    D = g.shape[1]
    n = T // tm
    row = lambda i: (i, 0)
    const = lambda i: (0, 0)
    dest3 = dest.reshape(n, 1, 2 * tm)
    return pl.pallas_call(
        _combine_kernel,
        grid=(n,),
        in_specs=[pl.BlockSpec((1, 1, 2 * tm), lambda i: (i, 0, 0), memory_space=pltpu.SMEM),
                  pl.BlockSpec((1, 1, 2 * tm), lambda i: (jnp.minimum(i + 1, n - 1), 0, 0),
                               memory_space=pltpu.SMEM),
                  pl.BlockSpec(memory_space=pl.ANY),
                  pl.BlockSpec((tm, LANES), row),
                  pl.BlockSpec((tm * ROW_TILES, LANES), row),
                  pl.BlockSpec((1, D), const),
                  pl.BlockSpec((1, D), const)],
        out_specs=[pl.BlockSpec((tm, D), row), pl.BlockSpec((tm, D), row)],
        out_shape=[jax.ShapeDtypeStruct((T, D), jnp.float32),
                   jax.ShapeDtypeStruct((T, D), _MXU_DTYPE)],
        scratch_shapes=[pltpu.VMEM((2, 2, tm * ROW_TILES, LANES), jnp.float32),
                        pltpu.SemaphoreType.DMA((2, 2))],
        compiler_params=_cparams(("arbitrary",)),
        name="combine_ln",
    )(dest3, dest3, y_pad, route, h, g, b)


def _spread64(wt):
    n, K = wt.shape[0] // IDX_D, wt.shape[1]
    wt = wt.reshape(n, 2, IDX_D // 2, K)
    wt = jnp.pad(wt, ((0, 0), (0, 0), (0, IDX_D // 2), (0, 0)))
    return wt.reshape(n * LANES, K)


_C_RQ = 0
_C_RV = 2 * RET_HEADS * RET_DK
_C_AQ = _C_RV + 2 * RET_HEADS * RET_DV
_C_AK = _C_AQ + ATT_HEADS * ATT_HD
_C_IQ = _C_AK + 2 * ATT_HD
_C_IK = _C_IQ + IDX_HEADS * IDX_D
_C_IW = _C_IK + IDX_D
_C_GR = _C_IW + IDX_HEADS
_C_END = _C_GR + 2 * D_MODEL_


def _moe_plan(route, counts):
    T = route.shape[0]
    n_assign = T * 2
    n_blocks = -(-n_assign // DISP_BLK) + N_EXP
    cnt = counts[0, N_GRP:N_GRP + N_EXP].astype(jnp.int32)
    padded = (cnt + DISP_BLK - 1) // DISP_BLK * DISP_BLK
    pad_ends = jnp.cumsum(padded)
    pad_starts = pad_ends - padded
    expert = route[:, 0:2].astype(jnp.int32)
    rank = route[:, 2:4].astype(jnp.int32)
    is_e = expert[:, :, None] == jnp.arange(N_EXP, dtype=jnp.int32)
    dest = (jnp.sum(jnp.where(is_e, pad_starts, 0), axis=-1) + rank).reshape(n_assign)
    tok = jnp.repeat(jnp.arange(T, dtype=jnp.int32), 2)
    tok_pad = jnp.zeros((n_blocks * DISP_BLK,), jnp.int32).at[dest].set(tok)
    block_start = jnp.arange(n_blocks, dtype=jnp.int32) * DISP_BLK
    block_expert = jnp.minimum(
        jnp.sum((pad_ends[None, :] <= block_start[:, None]).astype(jnp.int32), axis=1), N_EXP - 1)
    return dest, tok_pad, block_expert


def kernel(x, w_in, w_ret_o, w_dsa_o, w_out, ln1_g, ln1_b, router_grp_w, router_grp_b,
           router_exp_w, router_exp_b, w_gate, w_up, w_down, ln2_g, ln2_b):
    B, S, D = x.shape
    T = B * S
    cdt = _MXU_DTYPE
    assert w_in.shape[-1] == _C_END
    t_ret = [jnp.asarray(t) for t in _rot_tables(S, RET_DK, 4 * RET_DK, "half")]
    t_att = [jnp.asarray(t) for t in _rot_tables(S, ATT_HD, 8 * ATT_HD, "half")]
    t_idx = [jnp.asarray(t) for t in _rot_tables(S, IDX_D, 8 * LANES, "spread")]
    ident = (np.ones((S, LANES), np.float32), np.zeros((S, LANES), np.float32))
    t_e = [jnp.asarray(np.concatenate([ta[:, :LANES], i, ti[:, :LANES], i], axis=1))
           for ta, ti, i in zip(_rot_tables(S, ATT_HD, LANES, "half"),
                                _rot_tables(S, IDX_D, LANES, "spread"), ident)]
    xf = x.reshape(T, D)
    xb = xf.astype(cdt)
    w_in_t = jnp.swapaxes(w_in, 1, 2)
    for l in range(DEPTH_):
        w_iq = _spread64(w_in_t[l, _C_IQ:_C_IK])
        w_e = jnp.concatenate([w_in_t[l, _C_AK:_C_IQ], _spread64(w_in_t[l, _C_IK:_C_IW]),
                               jnp.pad(w_in_t[l, _C_IW:_C_GR], ((0, LANES - IDX_HEADS), (0, 0)))],
                              axis=0)
        w_gates = w_in_t[l, _C_GR:_C_END]
        qk = _proj(xb, w_in_t, B, S, layer=l, col0=_C_RQ, n_cols=_C_RV - _C_RQ, rot=RET_DK,
                   tables=t_ret, tm=512)
        vg = _proj(xb, w_in_t, B, S, layer=l, col0=_C_RV, n_cols=_C_AQ - _C_RV)
        aq = _proj(xb, w_in_t, B, S, layer=l, col0=_C_AQ, n_cols=_C_AK - _C_AQ, rot=ATT_HD,
                   tables=t_att, tm=512)
        iq = _proj(xb, w_iq, B, S, rot=LANES, tables=t_idx, tm=512)
        e = _proj(xb, w_e, B, S, rot=LANES, tables=t_e, out_dtype=jnp.float32, tm=512)
        gates = _proj(xb, w_gates, B, S)
        ret = _retention(qk, vg, B, S)
        att = _dsa(aq, iq, e, B, S)
        rw = jnp.pad(jnp.concatenate([router_grp_w[l], router_exp_w[l]], axis=1),
                     ((0, 0), (0, LANES - N_GRP - N_EXP)))
        rb = jnp.pad(jnp.concatenate([router_grp_b[l], router_exp_b[l]]),
                     (0, LANES - N_GRP - N_EXP)).reshape(1, LANES)
        h, logits = _merge(ret, att, gates, xf, w_ret_o[l].astype(cdt), w_dsa_o[l].astype(cdt),
                           w_out[l].astype(cdt), ln1_g[l].reshape(1, D), ln1_b[l].reshape(1, D),
                           rw, rb)
        route, counts = _route(logits)
        dest, tok_pad, block_expert = _moe_plan(route, counts)
        y_pad = _ffn(block_expert, tok_pad, h, w_gate, w_up, w_down, l)
        xf, xb = _combine(dest, y_pad, route, h, ln2_g[l].reshape(1, D), ln2_b[l].reshape(1, D))
    return xf.reshape(B, S, D)
```

```python
import functools

import numpy as np
import jax
import jax.numpy as jnp
from jax import lax
from jax.experimental import pallas as pl
from jax.experimental.pallas import tpu as pltpu

D_MODEL_ = 1024
DEPTH_ = 2
RET_HEADS = 4
RET_DK = 256
RET_DV = 512
RET_C = 128
ATT_HEADS = 8
ATT_HD = 128
IDX_HEADS = 8
IDX_D = 64
TOPK = 256
QBLK = 128
N_GRP = 4
EXP_PER_GRP = 8
N_EXP = N_GRP * EXP_PER_GRP
EXP_FF = 512
FFN_BLK = 256
THETA = 10000.0
LN_EPS_ = 1e-5
GN_EPS_ = 1e-5
ALPHA = (2 * DEPTH_) ** 0.25

LANES = 128
SUBLANES = 8
INT_MIN = -(2 ** 31)
_CODE_NEG_INF = INT_MIN + 0x7FFFFF
_F32_LOWEST = float(np.finfo(np.float32).min)

_MXU_DTYPE = jnp.bfloat16
_VMEM_LIMIT = 56 * 1024 * 1024

ROW_TILES = D_MODEL_ // LANES
assert ROW_TILES == SUBLANES


def _cparams(sem):
    return pltpu.CompilerParams(dimension_semantics=sem, vmem_limit_bytes=_VMEM_LIMIT)


def _dot(a, b):
    return jnp.dot(a, b, preferred_element_type=jnp.float32)


def _dot_nt(a, b):
    return lax.dot_general(a, b, (((1,), (1,)), ((), ())), preferred_element_type=jnp.float32)


def _dot_tn(a, b):
    return lax.dot_general(a, b, (((0,), (0,)), ((), ())), preferred_element_type=jnp.float32)


def _tile_rows(ref, n, s):
    return ref.at[pl.ds(s, n, stride=ROW_TILES), :]


def _load_tiled(ref, n):
    return jnp.concatenate([_tile_rows(ref, n, s)[...] for s in range(ROW_TILES)], axis=1)


def _store_tiled(ref, val):
    n = val.shape[0]
    for s in range(ROW_TILES):
        _tile_rows(ref, n, s)[...] = val[:, s * LANES:(s + 1) * LANES]


def _swap_halves(a, d):
    tn = a.shape[1]
    if d == 2 * LANES:
        parts = []
        for c in range(0, tn, d):
            parts += [a[:, c + LANES:c + d], a[:, c:c + LANES]]
        return jnp.concatenate(parts, axis=1)
    assert d == LANES
    return jnp.concatenate(
        [pltpu.roll(a[:, c:c + LANES], LANES // 2, axis=1) for c in range(0, tn, LANES)], axis=1)


def _proj_kernel(x_ref, w_ref, *rest, rot):
    if rot:
        c_ref, s_ref, o_ref, w_s = rest
    else:
        o_ref, w_s = rest

    @pl.when((pl.program_id(1) == 0) & (pl.program_id(2) == 0))
    def _():
        w_s[...] = w_ref[...].astype(w_s.dtype)

    acc = _dot_nt(x_ref[...], w_s[...])
    if rot:
        acc = acc * c_ref[...] + _swap_halves(acc, rot) * s_ref[...]
    o_ref[...] = acc.astype(o_ref.dtype)


def _proj(x, wt, batch, seq, *, layer=None, col0=0, n_cols=None, rot=0, tables=None,
          out_dtype=None, tm=1024, tn=1024):
    T, K = x.shape
    n_cols = n_cols or wt.shape[-2]
    tn = min(tn, n_cols)
    tm = min(tm, seq)
    n_seq = seq // tm
    assert col0 % tn == 0 and n_cols % tn == 0 and seq % tm == 0
    c0 = col0 // tn
    row = lambda j, p, b: (b * n_seq + p, 0)
    if layer is None:
        w_spec = pl.BlockSpec((tn, K), lambda j, p, b: (c0 + j, 0))
    else:
        w_spec = pl.BlockSpec((None, tn, K), lambda j, p, b: (layer, c0 + j, 0))
    in_specs = [pl.BlockSpec((tm, K), row), w_spec]
    args = [x, wt]
    if rot:
        in_specs += [pl.BlockSpec((tm, tn), lambda j, p, b: (p, 0))] * 2
        args += list(tables)
    return pl.pallas_call(
        functools.partial(_proj_kernel, rot=rot),
        grid=(n_cols // tn, n_seq, batch),
        in_specs=in_specs,
        out_specs=pl.BlockSpec((tm, tn), lambda j, p, b: (b * n_seq + p, j)),
        out_shape=jax.ShapeDtypeStruct((T, n_cols), out_dtype or x.dtype),
        scratch_shapes=[pltpu.VMEM((tn, K), _MXU_DTYPE)],
        compiler_params=_cparams(("parallel", "arbitrary", "arbitrary")),
        name="proj_rot%d" % rot,
    )(*args)


def _rot_tables(seq, d, width, layout):
    inv = 1.0 / (THETA ** (np.arange(0, d, 2, dtype=np.float64) / d))
    ang = np.arange(seq, dtype=np.float64)[:, None] * inv[None, :]
    cos, sin = np.cos(ang), np.sin(ang)
    if layout == "half":
        c = np.concatenate([cos, cos], axis=1)
        s = np.concatenate([-sin, sin], axis=1)
    else:
        z = np.zeros_like(cos)
        c = np.concatenate([cos, z, cos, z], axis=1)
        s = np.concatenate([-sin, z, sin, z], axis=1)
    reps = width // c.shape[1]
    return np.tile(c, (1, reps)).astype(np.float32), np.tile(s, (1, reps)).astype(np.float32)


_RET_CHUNKS_PER_STEP = 2


def _ret_kernel(qk_ref, v_ref, g_ref, dec_ref, xi_ref, zeta_ref, o_ref, r_ref, *, gamma_c):
    H, DK, DV, C = RET_HEADS, RET_DK, RET_DV, RET_C

    @pl.when(pl.program_id(1) == 0)
    def _():
        r_ref[...] = jnp.zeros_like(r_ref)

    for h in range(H):
        xi = jnp.concatenate([xi_ref[h]] * (DV // LANES), axis=1)
        zeta = jnp.concatenate([zeta_ref[h]] * (DV // LANES), axis=1)
        for c in range(qk_ref.shape[0] // C):
            rows = slice(c * C, (c + 1) * C)
            q = qk_ref[rows, h * DK:(h + 1) * DK]
            k = qk_ref[rows, (H + h) * DK:(H + h + 1) * DK]
            v = v_ref[rows, h * DV:(h + 1) * DV]
            s = _dot_nt(q, k) * dec_ref[h]
            intra = _dot(s.astype(_MXU_DTYPE), v)
            r_old = r_ref[h]
            cross = _dot(q, r_old.astype(_MXU_DTYPE))
            y = intra + cross * xi
            vz = (v.astype(jnp.float32) * zeta).astype(_MXU_DTYPE)
            r_ref[h] = gamma_c[h] * r_old + _dot_tn(k, vz)
            mu = jnp.mean(y, axis=-1, keepdims=True)
            yc = y - mu
            var = jnp.mean(yc * yc, axis=-1, keepdims=True)
            yn = yc * lax.rsqrt(var + GN_EPS_)
            g = g_ref[rows, h * DV:(h + 1) * DV].astype(jnp.float32)
            o_ref[rows, h * DV:(h + 1) * DV] = (yn * (g * jax.nn.sigmoid(g))).astype(o_ref.dtype)


def _retention(qk, vg, batch, seq):
    H, DK, DV, C = RET_HEADS, RET_DK, RET_DV, RET_C
    rows = C * _RET_CHUNKS_PER_STEP
    n_steps = seq // rows
    lg = np.log(1.0 - 2.0 ** (-5.0 - np.arange(H, dtype=np.float64)))
    pos = np.arange(C, dtype=np.float64)
    diff = pos[:, None] - pos[None, :]
    dec = np.where(diff >= 0, np.exp(np.maximum(diff, 0.0)[None] * lg[:, None, None]), 0.0)
    dec = (dec * DK ** -0.5).astype(np.float32)
    ones = np.ones((1, 1, LANES))
    xi = (np.exp((pos[None, :] + 1.0) * lg[:, None])[:, :, None] * ones).astype(np.float32)
    zeta = (np.exp((C - 1.0 - pos)[None, :] * lg[:, None])[:, :, None] * ones
            * DK ** -0.5).astype(np.float32)
    gamma_c = tuple(float(g) for g in np.exp(C * lg))
    T = qk.shape[0]
    const3 = lambda b, n: (0, 0, 0)
    row = lambda b, n: (b * n_steps + n, 0)
    return pl.pallas_call(
        functools.partial(_ret_kernel, gamma_c=gamma_c),
        grid=(batch, n_steps),
        in_specs=[pl.BlockSpec((rows, 2 * H * DK), row),
                  pl.BlockSpec((rows, H * DV), row),
                  pl.BlockSpec((rows, H * DV), lambda b, n: (b * n_steps + n, 1)),
                  pl.BlockSpec((H, C, C), const3),
                  pl.BlockSpec((H, C, LANES), const3),
                  pl.BlockSpec((H, C, LANES), const3)],
        out_specs=pl.BlockSpec((rows, H * DV), row),
        out_shape=jax.ShapeDtypeStruct((T, H * DV), qk.dtype),
        scratch_shapes=[pltpu.VMEM((H, DK, DV), jnp.float32)],
        compiler_params=_cparams(("parallel", "arbitrary")),
        name="retention",
    )(qk, vg, vg, jnp.asarray(dec), jnp.asarray(xi), jnp.asarray(zeta))


_COUNT_CHAINS = 8
_DSA_CLASSES = 8


def _col_count(mask):
    n, qb = mask.shape
    part = jnp.sum(mask.astype(jnp.int32).reshape(_COUNT_CHAINS, n // _COUNT_CHAINS, qb), axis=1)
    return jnp.sum(part, axis=0, keepdims=True)


def _dsa_block(sv, j, aq_ref, iq_ref, eq_ref, o_ref, k_s, ik_s, vt_s, score_s, bias_s, topk):
    QB = aq_ref.shape[0]
    w_t = eq_ref[:, 3 * LANES:4 * LANES].T * ((IDX_D ** -0.5) * (IDX_HEADS ** -0.5))
    score = jnp.zeros((sv, QB), jnp.float32)
    for h in range(IDX_HEADS):
        rel = jnp.maximum(_dot_nt(ik_s[0:sv, :], iq_ref[:, h * LANES:(h + 1) * LANES]), 0.0)
        score = score + rel * w_t[h:h + 1, :]
    key_pos = lax.broadcasted_iota(jnp.int32, (sv, QB), 0)
    q_pos = j * QB + lax.broadcasted_iota(jnp.int32, (sv, QB), 1)
    score_s[0:sv, :] = jnp.where(key_pos <= q_pos, score, -jnp.inf)

    def count_ge(c):
        return _col_count(score_s[0:sv, :] >= c)

    def float_of(code):
        return pltpu.bitcast(jnp.where(code < 0, code ^ 0x7FFFFFFF, code), jnp.float32)

    code = jnp.where(count_ge(jnp.zeros((1, QB), jnp.float32)) >= topk, 0, INT_MIN)

    def bit_step(i, code):
        cand = code | lax.shift_left(jnp.int32(1), 30 - i)
        ok = (count_ge(float_of(cand)) >= topk) | (cand <= _CODE_NEG_INF)
        return jnp.where(ok, cand, code)

    code = lax.fori_loop(0, 31, bit_step, code)
    thr = float_of(code)
    thr_lo = jnp.maximum(thr, _F32_LOWEST)
    has_tie = jnp.max(jnp.where(code > _CODE_NEG_INF, count_ge(thr) - topk, 0)) > 0

    @pl.when(jnp.logical_not(has_tie))
    def _():
        bias_s[0:sv, :] = jnp.where(score_s[0:sv, :] >= thr_lo, 0.0, -jnp.inf)

    @pl.when(has_tie)
    def _():
        need_f = (topk - _col_count(score_s[0:sv, :] > thr)).astype(jnp.float32)
        r_i = lax.broadcasted_iota(jnp.int32, (LANES, LANES), 0)
        c_i = lax.broadcasted_iota(jnp.int32, (LANES, LANES), 1)
        lower = (r_i > c_i).astype(_MXU_DTYPE)
        carry = jnp.zeros((1, QB), jnp.float32)
        for c in range(sv // LANES):
            sl = slice(c * LANES, (c + 1) * LANES)
            sc = score_s[sl, :]
            eq_c = sc == thr
            eq_f = eq_c.astype(jnp.float32)
            before = _dot(lower, eq_f.astype(_MXU_DTYPE)) + carry
            take = ((sc > thr) | (eq_c & (before < need_f))) & (sc >= _F32_LOWEST)
            bias_s[sl, :] = jnp.where(take, 0.0, -jnp.inf)
            carry = carry + jnp.sum(eq_f, axis=0, keepdims=True)

    c_exp = (ATT_HD ** -0.5) * float(np.log2(np.e))
    for h in range(ATT_HEADS):
        s = _dot_nt(k_s[0:sv, :], aq_ref[:, h * ATT_HD:(h + 1) * ATT_HD]) + bias_s[0:sv, :]
        m = jnp.max(s, axis=0, keepdims=True)
        p = jnp.exp2((s - m) * c_exp)
        l = jnp.sum(p, axis=0, keepdims=True)
        o_t = _dot(vt_s[:, 0:sv], p.astype(_MXU_DTYPE)) / l
        o_ref[:, h * ATT_HD:(h + 1) * ATT_HD] = o_t.T.astype(o_ref.dtype)


def _dsa_kernel(aq_ref, iq_ref, eq_ref, ek_ref, o_ref, k_s, ik_s, vt_s, score_s, bias_s, *, topk):
    S = ek_ref.shape[0]
    QB = aq_ref.shape[0]
    j = pl.program_id(1)

    @pl.when(j == 0)
    def _():
        k_s[...] = ek_ref[:, 0:LANES].astype(k_s.dtype)
        vt_s[...] = ek_ref[:, LANES:2 * LANES].T.astype(vt_s.dtype)
        ik_s[...] = ek_ref[:, 2 * LANES:3 * LANES].astype(ik_s.dtype)

    n_blk = S // QB
    n_cls = min(_DSA_CLASSES, n_blk)
    per_class = n_blk // n_cls
    for c in range(n_cls):
        @pl.when((j >= c * per_class) & (j < (c + 1) * per_class))
        def _(c=c):
            _dsa_block((c + 1) * per_class * QB, j, aq_ref, iq_ref, eq_ref, o_ref,
                       k_s, ik_s, vt_s, score_s, bias_s, topk)


def _dsa(aq, iq, e, batch, seq):
    T = aq.shape[0]
    nb = seq // QBLK
    topk = min(TOPK, seq // 4)
    qrow = lambda b, j: (b * nb + j, 0)
    return pl.pallas_call(
        functools.partial(_dsa_kernel, topk=topk),
        grid=(batch, nb),
        in_specs=[pl.BlockSpec((QBLK, ATT_HEADS * ATT_HD), qrow),
                  pl.BlockSpec((QBLK, IDX_HEADS * LANES), qrow),
                  pl.BlockSpec((QBLK, 4 * LANES), qrow),
                  pl.BlockSpec((seq, 4 * LANES), lambda b, j: (b, 0))],
        out_specs=pl.BlockSpec((QBLK, ATT_HEADS * ATT_HD), qrow),
        out_shape=jax.ShapeDtypeStruct((T, ATT_HEADS * ATT_HD), aq.dtype),
        scratch_shapes=[pltpu.VMEM((seq, LANES), aq.dtype),
                        pltpu.VMEM((seq, LANES), aq.dtype),
                        pltpu.VMEM((LANES, seq), aq.dtype),
                        pltpu.VMEM((seq, QBLK), jnp.float32),
                        pltpu.VMEM((seq, QBLK), jnp.float32)],
        compiler_params=_cparams(("parallel", "arbitrary")),
        name="dsa",
    )(aq, iq, e, e)


def _layer_norm(z, g, b):
    mu = jnp.mean(z, axis=-1, keepdims=True)
    zc = z - mu
    var = jnp.mean(zc * zc, axis=-1, keepdims=True)
    return zc * lax.rsqrt(var + LN_EPS_) * g + b


def _split_hi_lo(a):
    hi = a.astype(_MXU_DTYPE)
    lo = (a - hi.astype(jnp.float32)).astype(_MXU_DTYPE)
    return hi, lo


def _merge_kernel(ret_ref, att_ref, gr_ref, ga_ref, x_ref, wr_ref, wd_ref, wo_ref, g_ref, b_ref,
                  rwh_ref, rwl_ref, rb_ref, h_ref, route_ref, cnt_ref, carry_s):
    @pl.when(pl.program_id(0) == 0)
    def _():
        carry_s[...] = jnp.zeros_like(carry_s)

    a = _dot(ret_ref[...], wr_ref[...])
    d = _dot(att_ref[...], wd_ref[...])
    merged = (jax.nn.sigmoid(gr_ref[...].astype(jnp.float32)) * a
              + jax.nn.sigmoid(ga_ref[...].astype(jnp.float32)) * d)
    mix = _dot(merged.astype(_MXU_DTYPE), wo_ref[...])
    h = _layer_norm(ALPHA * x_ref[...] + mix, g_ref[...], b_ref[...])
    _store_tiled(h_ref, h)
    h_hi, h_lo = _split_hi_lo(h)
    logits = (_dot(h_hi, rwh_ref[...]) + _dot(h_lo, rwh_ref[...]) + _dot(h_hi, rwl_ref[...])
              + rb_ref[...])
    route_ref[...] = _route_tile(logits, carry_s)
    cnt_ref[...] = carry_s[...]


def _merge(ret, att, gates, x, wr, wd, wo, g, b, rw, rb, tm=512):
    T, D = x.shape
    row = lambda i: (i, 0)
    const = lambda i: (0, 0)
    rw_hi, rw_lo = _split_hi_lo(rw)
    return pl.pallas_call(
        _merge_kernel,
        grid=(T // tm,),
        in_specs=[pl.BlockSpec((tm, ret.shape[1]), row),
                  pl.BlockSpec((tm, att.shape[1]), row),
                  pl.BlockSpec((tm, D), lambda i: (i, 0)),
                  pl.BlockSpec((tm, D), lambda i: (i, 1)),
                  pl.BlockSpec((tm, D), row),
                  pl.BlockSpec(wr.shape, const),
                  pl.BlockSpec(wd.shape, const),
                  pl.BlockSpec(wo.shape, const),
                  pl.BlockSpec((1, D), const),
                  pl.BlockSpec((1, D), const),
                  pl.BlockSpec(rw.shape, const),
                  pl.BlockSpec(rw.shape, const),
                  pl.BlockSpec((1, LANES), const)],
        out_specs=[pl.BlockSpec((tm * ROW_TILES, LANES), row), pl.BlockSpec((tm, LANES), row),
                   pl.BlockSpec((SUBLANES, LANES), const)],
        out_shape=[jax.ShapeDtypeStruct((T * ROW_TILES, LANES), jnp.float32),
                   jax.ShapeDtypeStruct((T, LANES), jnp.float32),
                   jax.ShapeDtypeStruct((SUBLANES, LANES), jnp.float32)],
        scratch_shapes=[pltpu.VMEM((SUBLANES, LANES), jnp.float32)],
        compiler_params=_cparams(("arbitrary",)),
        name="merge_ln_router",
    )(ret, att, gates, gates, x, wr, wd, wo, g, b, rw_hi, rw_lo, rb)


def _route_tile(lg, carry_s):
    tm = lg.shape[0]
    lane = lax.broadcasted_iota(jnp.int32, (tm, LANES), 1).astype(jnp.float32)
    neg = -jnp.inf

    def first_max(v):
        m = jnp.max(v, axis=-1, keepdims=True)
        idx = jnp.min(jnp.where(v == m, lane, float(LANES)), axis=-1, keepdims=True)
        return m, idx

    gl = jnp.where(lane < N_GRP, lg, neg)
    gmax, gsel = first_max(gl)
    p_grp = 1.0 / jnp.sum(jnp.exp(gl - gmax), axis=-1, keepdims=True)
    lo = N_GRP + gsel * EXP_PER_GRP
    el = jnp.where((lane >= lo) & (lane < lo + EXP_PER_GRP), lg, neg)
    top1, i1 = first_max(el)
    top2, i2 = first_max(jnp.where(lane == i1, neg, el))
    z = jnp.exp(top2 - top1)
    g1 = p_grp / (1.0 + z)
    g2 = p_grp * z / (1.0 + z)
    oh1 = lane == i1
    oh2 = lane == i2
    oh = (oh1 | oh2).astype(jnp.float32)
    r_i = lax.broadcasted_iota(jnp.int32, (tm, tm), 0)
    c_i = lax.broadcasted_iota(jnp.int32, (tm, tm), 1)
    lower = (r_i > c_i).astype(_MXU_DTYPE)
    before = _dot(lower, oh.astype(_MXU_DTYPE)) + carry_s[0:1, :]
    rank1 = jnp.sum(jnp.where(oh1, before, 0.0), axis=-1, keepdims=True)
    rank2 = jnp.sum(jnp.where(oh2, before, 0.0), axis=-1, keepdims=True)
    new_carry = carry_s[0:1, :] + jnp.sum(oh, axis=0, keepdims=True)
    carry_s[...] = jnp.broadcast_to(new_carry, carry_s.shape)
    e1 = i1 - N_GRP
    e2 = i2 - N_GRP
    out = jnp.zeros((tm, LANES), jnp.float32)
    for col, val in enumerate((e1, e2, rank1, rank2, g1, g2)):
        out = jnp.where(lane == col, val, out)
    return out


_GATHER_UNROLL = 8


def _token_copy(src_hbm, dst_vmem, sem, src_tok, dst_tok):
    src = src_hbm.at[pl.ds(pl.multiple_of(src_tok * ROW_TILES, ROW_TILES), ROW_TILES)]
    dst = dst_vmem.at[pl.ds(pl.multiple_of(dst_tok * ROW_TILES, ROW_TILES), ROW_TILES)]
    return pltpu.make_async_copy(src, dst, sem)


def _start_token_gather(idx_ref, src_hbm, dst_vmem, sem, n, stride=1, offset=0):
    def issue(r, c):
        _token_copy(src_hbm, dst_vmem, sem, idx_ref[0, 0, offset + r * stride], r).start()
        return c

    lax.fori_loop(0, n, issue, 0, unroll=_GATHER_UNROLL)


def _wait_token_gather(src_hbm, dst_vmem, sem, n):
    pltpu.make_async_copy(src_hbm.at[pl.ds(0, n * ROW_TILES)], dst_vmem, sem).wait()


def _ffn_kernel(be_ref, tok_ref, tok_next_ref, h_hbm, wg_ref, wu_ref, wd_ref, y_ref,
                x_buf, wg_s, wu_s, wd_s, sem):
    i = pl.program_id(0)
    n = pl.num_programs(0)
    slot = i % 2

    @pl.when(i == 0)
    def _():
        _start_token_gather(tok_ref, h_hbm, x_buf.at[0], sem.at[0], FFN_BLK)

    @pl.when(i + 1 < n)
    def _():
        _start_token_gather(tok_next_ref, h_hbm, x_buf.at[1 - slot], sem.at[1 - slot], FFN_BLK)

    @pl.when((i == 0) | (be_ref[i] != be_ref[jnp.maximum(i - 1, 0)]))
    def _():
        wg_s[...] = wg_ref[0].astype(wg_s.dtype)
        wu_s[...] = wu_ref[0].astype(wu_s.dtype)
        wd_s[...] = wd_ref[0].astype(wd_s.dtype)

    _wait_token_gather(h_hbm, x_buf.at[slot], sem.at[slot], FFN_BLK)
    xb = _load_tiled(x_buf.at[slot], FFN_BLK).astype(_MXU_DTYPE)
    gate = _dot(xb, wg_s[...])
    up = _dot(xb, wu_s[...])
    act = (gate * jax.nn.sigmoid(gate)) * up
    _store_tiled(y_ref, _dot(act.astype(_MXU_DTYPE), wd_s[...]))


def _ffn(block_expert, tok_pad, h, wg, wu, wd, layer):
    n_blocks = block_expert.shape[0]
    D, F = wg.shape[2], wg.shape[3]
    wmap = lambda i, be: (layer, be[i], 0, 0)
    tok3 = tok_pad.reshape(n_blocks, 1, FFN_BLK)
    blk_rows = FFN_BLK * ROW_TILES
    grid_spec = pltpu.PrefetchScalarGridSpec(
        num_scalar_prefetch=1,
        grid=(n_blocks,),
        in_specs=[pl.BlockSpec((1, 1, FFN_BLK), lambda i, be: (i, 0, 0), memory_space=pltpu.SMEM),
                  pl.BlockSpec((1, 1, FFN_BLK), lambda i, be: (jnp.minimum(i + 1, n_blocks - 1), 0, 0),
                               memory_space=pltpu.SMEM),
                  pl.BlockSpec(memory_space=pl.ANY),
                  pl.BlockSpec((None, 1, D, F), wmap),
                  pl.BlockSpec((None, 1, D, F), wmap),
                  pl.BlockSpec((None, 1, F, D), wmap)],
        out_specs=pl.BlockSpec((blk_rows, LANES), lambda i, be: (i, 0)),
        scratch_shapes=[pltpu.VMEM((2, blk_rows, LANES), jnp.float32),
                        pltpu.VMEM((D, F), _MXU_DTYPE),
                        pltpu.VMEM((D, F), _MXU_DTYPE),
                        pltpu.VMEM((F, D), _MXU_DTYPE),
                        pltpu.SemaphoreType.DMA((2,))],
    )
    return pl.pallas_call(
        _ffn_kernel,
        grid_spec=grid_spec,
        out_shape=jax.ShapeDtypeStruct((n_blocks * blk_rows, LANES), jnp.float32),
        compiler_params=_cparams(("arbitrary",)),
        name="expert_ffn",
    )(block_expert, tok3, tok3, h, wg, wu, wd)


def _combine_kernel(dest_ref, dest_next_ref, y_hbm, r_ref, h_ref, g_ref, b_ref, o_ref, ob_ref,
                    y_buf, sem):
    tm = o_ref.shape[0]
    i = pl.program_id(0)
    n = pl.num_programs(0)
    slot = i % 2

    def start(idx_ref, s):
        for c in range(2):
            _start_token_gather(idx_ref, y_hbm, y_buf.at[s, c], sem.at[s, c], tm, stride=2, offset=c)

    @pl.when(i == 0)
    def _():
        start(dest_ref, 0)

    @pl.when(i + 1 < n)
    def _():
        start(dest_next_ref, 1 - slot)

    for c in range(2):
        _wait_token_gather(y_hbm, y_buf.at[slot, c], sem.at[slot, c], tm)
    r = r_ref[...]
    g0 = r[:, 4:5]
    g1 = r[:, 5:6]
    z = jnp.concatenate(
        [ALPHA * _tile_rows(h_ref, tm, s)[...]
         + _tile_rows(y_buf.at[slot, 0], tm, s)[...] * g0
         + _tile_rows(y_buf.at[slot, 1], tm, s)[...] * g1 for s in range(ROW_TILES)], axis=1)
    out = _layer_norm(z, g_ref[...], b_ref[...])
    o_ref[...] = out
    ob_ref[...] = out.astype(ob_ref.dtype)


def _combine(dest, y_pad, route, h, g, b, tm=128):
    T = route.shape[0]
    D = g.shape[1]
    n = T // tm
    row = lambda i: (i, 0)
    const = lambda i: (0, 0)
    dest3 = dest.reshape(n, 1, 2 * tm)
    return pl.pallas_call(
        _combine_kernel,
        grid=(n,),
        in_specs=[pl.BlockSpec((1, 1, 2 * tm), lambda i: (i, 0, 0), memory_space=pltpu.SMEM),
                  pl.BlockSpec((1, 1, 2 * tm), lambda i: (jnp.minimum(i + 1, n - 1), 0, 0),
                               memory_space=pltpu.SMEM),
                  pl.BlockSpec(memory_space=pl.ANY),
                  pl.BlockSpec((tm, LANES), row),
                  pl.BlockSpec((tm * ROW_TILES, LANES), row),
                  pl.BlockSpec((1, D), const),
                  pl.BlockSpec((1, D), const)],
        out_specs=[pl.BlockSpec((tm, D), row), pl.BlockSpec((tm, D), row)],
        out_shape=[jax.ShapeDtypeStruct((T, D), jnp.float32),
                   jax.ShapeDtypeStruct((T, D), _MXU_DTYPE)],
        scratch_shapes=[pltpu.VMEM((2, 2, tm * ROW_TILES, LANES), jnp.float32),
                        pltpu.SemaphoreType.DMA((2, 2))],
        compiler_params=_cparams(("arbitrary",)),
        name="combine_ln",
    )(dest3, dest3, y_pad, route, h, g, b)


def _spread64(wt):
    n, K = wt.shape[0] // IDX_D, wt.shape[1]
    wt = wt.reshape(n, 2, IDX_D // 2, K)
    wt = jnp.pad(wt, ((0, 0), (0, 0), (0, IDX_D // 2), (0, 0)))
    return wt.reshape(n * LANES, K)


_C_RQ = 0
_C_RV = 2 * RET_HEADS * RET_DK
_C_AQ = _C_RV + 2 * RET_HEADS * RET_DV
_C_AK = _C_AQ + ATT_HEADS * ATT_HD
_C_IQ = _C_AK + 2 * ATT_HD
_C_IK = _C_IQ + IDX_HEADS * IDX_D
_C_IW = _C_IK + IDX_D
_C_GR = _C_IW + IDX_HEADS
_C_END = _C_GR + 2 * D_MODEL_


def _moe_plan(route, counts):
    T = route.shape[0]
    n_assign = T * 2
    n_blocks = -(-n_assign // FFN_BLK) + N_EXP
    cnt = counts[0, N_GRP:N_GRP + N_EXP].astype(jnp.int32)
    padded = (cnt + FFN_BLK - 1) // FFN_BLK * FFN_BLK
    pad_ends = jnp.cumsum(padded)
    pad_starts = pad_ends - padded
    expert = route[:, 0:2].astype(jnp.int32)
    rank = route[:, 2:4].astype(jnp.int32)
    is_e = expert[:, :, None] == jnp.arange(N_EXP, dtype=jnp.int32)
    dest = (jnp.sum(jnp.where(is_e, pad_starts, 0), axis=-1) + rank).reshape(n_assign)
    tok = jnp.repeat(jnp.arange(T, dtype=jnp.int32), 2)
    tok_pad = jnp.zeros((n_blocks * FFN_BLK,), jnp.int32).at[dest].set(tok)
    block_start = jnp.arange(n_blocks, dtype=jnp.int32) * FFN_BLK
    block_expert = jnp.minimum(
        jnp.sum((pad_ends[None, :] <= block_start[:, None]).astype(jnp.int32), axis=1), N_EXP - 1)
    return dest, tok_pad, block_expert


def kernel(x, w_in, w_ret_o, w_dsa_o, w_out, ln1_g, ln1_b, router_grp_w, router_grp_b,
           router_exp_w, router_exp_b, w_gate, w_up, w_down, ln2_g, ln2_b):
    B, S, D = x.shape
    T = B * S
    cdt = _MXU_DTYPE
    assert w_in.shape[-1] == _C_END
    t_ret = [jnp.asarray(t) for t in _rot_tables(S, RET_DK, 4 * RET_DK, "half")]
    t_att = [jnp.asarray(t) for t in _rot_tables(S, ATT_HD, 8 * ATT_HD, "half")]
    t_idx = [jnp.asarray(t) for t in _rot_tables(S, IDX_D, 8 * LANES, "spread")]
    ident = (np.ones((S, LANES), np.float32), np.zeros((S, LANES), np.float32))
    t_e = [jnp.asarray(np.concatenate([ta[:, :LANES], i, ti[:, :LANES], i], axis=1))
           for ta, ti, i in zip(_rot_tables(S, ATT_HD, LANES, "half"),
                                _rot_tables(S, IDX_D, LANES, "spread"), ident)]
    xf = x.reshape(T, D)
    xb = xf.astype(cdt)
    w_in_t = jnp.swapaxes(w_in, 1, 2)
    for l in range(DEPTH_):
        w_iq = _spread64(w_in_t[l, _C_IQ:_C_IK])
        w_e = jnp.concatenate([w_in_t[l, _C_AK:_C_IQ], _spread64(w_in_t[l, _C_IK:_C_IW]),
                               jnp.pad(w_in_t[l, _C_IW:_C_GR], ((0, LANES - IDX_HEADS), (0, 0)))],
                              axis=0)
        w_gates = w_in_t[l, _C_GR:_C_END]
        qk = _proj(xb, w_in_t, B, S, layer=l, col0=_C_RQ, n_cols=_C_RV - _C_RQ, rot=RET_DK,
                   tables=t_ret, tm=512)
        vg = _proj(xb, w_in_t, B, S, layer=l, col0=_C_RV, n_cols=_C_AQ - _C_RV)
        aq = _proj(xb, w_in_t, B, S, layer=l, col0=_C_AQ, n_cols=_C_AK - _C_AQ, rot=ATT_HD,
                   tables=t_att, tm=512)
        iq = _proj(xb, w_iq, B, S, rot=LANES, tables=t_idx, tm=512)
        e = _proj(xb, w_e, B, S, rot=LANES, tables=t_e, out_dtype=jnp.float32, tm=512)
        gates = _proj(xb, w_gates, B, S)
        ret = _retention(qk, vg, B, S)
        att = _dsa(aq, iq, e, B, S)
        rw = jnp.pad(jnp.concatenate([router_grp_w[l], router_exp_w[l]], axis=1),
                     ((0, 0), (0, LANES - N_GRP - N_EXP)))
        rb = jnp.pad(jnp.concatenate([router_grp_b[l], router_exp_b[l]]),
                     (0, LANES - N_GRP - N_EXP)).reshape(1, LANES)
        h, route, counts = _merge(ret, att, gates, xf, w_ret_o[l].astype(cdt),
                                  w_dsa_o[l].astype(cdt), w_out[l].astype(cdt),
                                  ln1_g[l].reshape(1, D), ln1_b[l].reshape(1, D), rw, rb)
        dest, tok_pad, block_expert = _moe_plan(route, counts)
        y_pad = _ffn(block_expert, tok_pad, h, w_gate, w_up, w_down, l)
        xf, xb = _combine(dest, y_pad, route, h, ln2_g[l].reshape(1, D), ln2_b[l].reshape(1, D))
    return xf.reshape(B, S, D)
```

```python
import functools

import numpy as np
import jax
import jax.numpy as jnp
from jax import lax
from jax.experimental import pallas as pl
from jax.experimental.pallas import tpu as pltpu

D_MODEL_ = 1024
DEPTH_ = 2
RET_HEADS = 4
RET_DK = 256
RET_DV = 512
RET_C = 128
ATT_HEADS = 8
ATT_HD = 128
IDX_HEADS = 8
IDX_D = 64
TOPK = 256
QBLK = 128
N_GRP = 4
EXP_PER_GRP = 8
N_EXP = N_GRP * EXP_PER_GRP
EXP_FF = 512
FFN_BLK = 128
THETA = 10000.0
LN_EPS_ = 1e-5
GN_EPS_ = 1e-5
ALPHA = (2 * DEPTH_) ** 0.25

LANES = 128
SUBLANES = 8
INT_MIN = -(2 ** 31)
_CODE_NEG_INF = INT_MIN + 0x7FFFFF
_F32_LOWEST = float(np.finfo(np.float32).min)

_MXU_DTYPE = jnp.bfloat16
_VMEM_LIMIT = 56 * 1024 * 1024

ROW_TILES = D_MODEL_ // LANES
assert ROW_TILES == SUBLANES


def _cparams(sem):
    return pltpu.CompilerParams(dimension_semantics=sem, vmem_limit_bytes=_VMEM_LIMIT)


def _dot(a, b):
    return jnp.dot(a, b, preferred_element_type=jnp.float32)


def _dot_nt(a, b):
    return lax.dot_general(a, b, (((1,), (1,)), ((), ())), preferred_element_type=jnp.float32)


def _dot_tn(a, b):
    return lax.dot_general(a, b, (((0,), (0,)), ((), ())), preferred_element_type=jnp.float32)


def _tile_rows(ref, n, s):
    return ref.at[pl.ds(s, n, stride=ROW_TILES), :]


def _load_tiled(ref, n):
    return jnp.concatenate([_tile_rows(ref, n, s)[...] for s in range(ROW_TILES)], axis=1)


def _store_tiled(ref, val):
    n = val.shape[0]
    for s in range(ROW_TILES):
        _tile_rows(ref, n, s)[...] = val[:, s * LANES:(s + 1) * LANES]


def _swap_halves(a, d):
    tn = a.shape[1]
    if d == 2 * LANES:
        parts = []
        for c in range(0, tn, d):
            parts += [a[:, c + LANES:c + d], a[:, c:c + LANES]]
        return jnp.concatenate(parts, axis=1)
    assert d == LANES
    return jnp.concatenate(
        [pltpu.roll(a[:, c:c + LANES], LANES // 2, axis=1) for c in range(0, tn, LANES)], axis=1)


def _proj_kernel(x_ref, w_ref, *rest, rot):
    if rot:
        c_ref, s_ref, o_ref, w_s = rest
    else:
        o_ref, w_s = rest

    @pl.when((pl.program_id(1) == 0) & (pl.program_id(2) == 0))
    def _():
        w_s[...] = w_ref[...].astype(w_s.dtype)

    acc = _dot_nt(x_ref[...], w_s[...])
    if rot:
        acc = acc * c_ref[...] + _swap_halves(acc, rot) * s_ref[...]
    o_ref[...] = acc.astype(o_ref.dtype)


def _proj(x, wt, batch, seq, *, layer=None, col0=0, n_cols=None, rot=0, tables=None,
          out_dtype=None, tm=1024, tn=1024):
    T, K = x.shape
    n_cols = n_cols or wt.shape[-2]
    tn = min(tn, n_cols)
    tm = min(tm, seq)
    n_seq = seq // tm
    assert col0 % tn == 0 and n_cols % tn == 0 and seq % tm == 0
    c0 = col0 // tn
    row = lambda j, p, b: (b * n_seq + p, 0)
    if layer is None:
        w_spec = pl.BlockSpec((tn, K), lambda j, p, b: (c0 + j, 0))
    else:
        w_spec = pl.BlockSpec((None, tn, K), lambda j, p, b: (layer, c0 + j, 0))
    in_specs = [pl.BlockSpec((tm, K), row), w_spec]
    args = [x, wt]
    if rot:
        in_specs += [pl.BlockSpec((tm, tn), lambda j, p, b: (p, 0))] * 2
        args += list(tables)
    return pl.pallas_call(
        functools.partial(_proj_kernel, rot=rot),
        grid=(n_cols // tn, n_seq, batch),
        in_specs=in_specs,
        out_specs=pl.BlockSpec((tm, tn), lambda j, p, b: (b * n_seq + p, j)),
        out_shape=jax.ShapeDtypeStruct((T, n_cols), out_dtype or x.dtype),
        scratch_shapes=[pltpu.VMEM((tn, K), _MXU_DTYPE)],
        compiler_params=_cparams(("parallel", "arbitrary", "arbitrary")),
        name="proj_rot%d" % rot,
    )(*args)


def _rot_tables(seq, d, width, layout):
    inv = 1.0 / (THETA ** (np.arange(0, d, 2, dtype=np.float64) / d))
    ang = np.arange(seq, dtype=np.float64)[:, None] * inv[None, :]
    cos, sin = np.cos(ang), np.sin(ang)
    if layout == "half":
        c = np.concatenate([cos, cos], axis=1)
        s = np.concatenate([-sin, sin], axis=1)
    else:
        z = np.zeros_like(cos)
        c = np.concatenate([cos, z, cos, z], axis=1)
        s = np.concatenate([-sin, z, sin, z], axis=1)
    reps = width // c.shape[1]
    return np.tile(c, (1, reps)).astype(np.float32), np.tile(s, (1, reps)).astype(np.float32)


_RET_CHUNKS_PER_STEP = 2


def _ret_kernel(qk_ref, v_ref, g_ref, dec_ref, xi_ref, zeta_ref, o_ref, r_ref, *, gamma_c):
    H, DK, DV, C = RET_HEADS, RET_DK, RET_DV, RET_C

    @pl.when(pl.program_id(1) == 0)
    def _():
        r_ref[...] = jnp.zeros_like(r_ref)

    for h in range(H):
        xi = jnp.concatenate([xi_ref[h]] * (DV // LANES), axis=1)
        zeta = jnp.concatenate([zeta_ref[h]] * (DV // LANES), axis=1)
        for c in range(qk_ref.shape[0] // C):
            rows = slice(c * C, (c + 1) * C)
            q = qk_ref[rows, h * DK:(h + 1) * DK]
            k = qk_ref[rows, (H + h) * DK:(H + h + 1) * DK]
            v = v_ref[rows, h * DV:(h + 1) * DV]
            s = _dot_nt(q, k) * dec_ref[h]
            intra = _dot(s.astype(_MXU_DTYPE), v)
            r_old = r_ref[h]
            cross = _dot(q, r_old.astype(_MXU_DTYPE))
            y = intra + cross * xi
            vz = (v.astype(jnp.float32) * zeta).astype(_MXU_DTYPE)
            r_ref[h] = gamma_c[h] * r_old + _dot_tn(k, vz)
            mu = jnp.mean(y, axis=-1, keepdims=True)
            yc = y - mu
            var = jnp.mean(yc * yc, axis=-1, keepdims=True)
            yn = yc * lax.rsqrt(var + GN_EPS_)
            g = g_ref[rows, h * DV:(h + 1) * DV].astype(jnp.float32)
            o_ref[rows, h * DV:(h + 1) * DV] = (yn * (g * jax.nn.sigmoid(g))).astype(o_ref.dtype)


def _retention(qk, vg, batch, seq):
    H, DK, DV, C = RET_HEADS, RET_DK, RET_DV, RET_C
    rows = C * _RET_CHUNKS_PER_STEP
    n_steps = seq // rows
    lg = np.log(1.0 - 2.0 ** (-5.0 - np.arange(H, dtype=np.float64)))
    pos = np.arange(C, dtype=np.float64)
    diff = pos[:, None] - pos[None, :]
    dec = np.where(diff >= 0, np.exp(np.maximum(diff, 0.0)[None] * lg[:, None, None]), 0.0)
    dec = (dec * DK ** -0.5).astype(np.float32)
    ones = np.ones((1, 1, LANES))
    xi = (np.exp((pos[None, :] + 1.0) * lg[:, None])[:, :, None] * ones).astype(np.float32)
    zeta = (np.exp((C - 1.0 - pos)[None, :] * lg[:, None])[:, :, None] * ones
            * DK ** -0.5).astype(np.float32)
    gamma_c = tuple(float(g) for g in np.exp(C * lg))
    T = qk.shape[0]
    const3 = lambda b, n: (0, 0, 0)
    row = lambda b, n: (b * n_steps + n, 0)
    return pl.pallas_call(
        functools.partial(_ret_kernel, gamma_c=gamma_c),
        grid=(batch, n_steps),
        in_specs=[pl.BlockSpec((rows, 2 * H * DK), row),
                  pl.BlockSpec((rows, H * DV), row),
                  pl.BlockSpec((rows, H * DV), lambda b, n: (b * n_steps + n, 1)),
                  pl.BlockSpec((H, C, C), const3),
                  pl.BlockSpec((H, C, LANES), const3),
                  pl.BlockSpec((H, C, LANES), const3)],
        out_specs=pl.BlockSpec((rows, H * DV), row),
        out_shape=jax.ShapeDtypeStruct((T, H * DV), qk.dtype),
        scratch_shapes=[pltpu.VMEM((H, DK, DV), jnp.float32)],
        compiler_params=_cparams(("parallel", "arbitrary")),
        name="retention",
    )(qk, vg, vg, jnp.asarray(dec), jnp.asarray(xi), jnp.asarray(zeta))


_COUNT_CHAINS = 8
_DSA_CLASSES = 8


def _col_count(mask):
    n, qb = mask.shape
    part = jnp.sum(mask.astype(jnp.int32).reshape(_COUNT_CHAINS, n // _COUNT_CHAINS, qb), axis=1)
    return jnp.sum(part, axis=0, keepdims=True)


def _dsa_block(sv, j, aq_ref, iq_ref, eq_ref, o_ref, k_s, ik_s, vt_s, score_s, bias_s, topk):
    QB = aq_ref.shape[0]
    w_t = eq_ref[:, 3 * LANES:4 * LANES].T * ((IDX_D ** -0.5) * (IDX_HEADS ** -0.5))
    score = jnp.zeros((sv, QB), jnp.float32)
    for h in range(IDX_HEADS):
        rel = jnp.maximum(_dot_nt(ik_s[0:sv, :], iq_ref[:, h * LANES:(h + 1) * LANES]), 0.0)
        score = score + rel * w_t[h:h + 1, :]
    key_pos = lax.broadcasted_iota(jnp.int32, (sv, QB), 0)
    q_pos = j * QB + lax.broadcasted_iota(jnp.int32, (sv, QB), 1)
    score_s[0:sv, :] = jnp.where(key_pos <= q_pos, score, -jnp.inf)

    def count_ge(c):
        return _col_count(score_s[0:sv, :] >= c)

    def float_of(code):
        return pltpu.bitcast(jnp.where(code < 0, code ^ 0x7FFFFFFF, code), jnp.float32)

    code = jnp.where(count_ge(jnp.zeros((1, QB), jnp.float32)) >= topk, 0, INT_MIN)

    def bit_step(i, code):
        cand = code | lax.shift_left(jnp.int32(1), 30 - i)
        ok = (count_ge(float_of(cand)) >= topk) | (cand <= _CODE_NEG_INF)
        return jnp.where(ok, cand, code)

    code = lax.fori_loop(0, 31, bit_step, code)
    thr = float_of(code)
    thr_lo = jnp.maximum(thr, _F32_LOWEST)
    has_tie = jnp.max(jnp.where(code > _CODE_NEG_INF, count_ge(thr) - topk, 0)) > 0

    @pl.when(jnp.logical_not(has_tie))
    def _():
        bias_s[0:sv, :] = jnp.where(score_s[0:sv, :] >= thr_lo, 0.0, -jnp.inf)

    @pl.when(has_tie)
    def _():
        need_f = (topk - _col_count(score_s[0:sv, :] > thr)).astype(jnp.float32)
        r_i = lax.broadcasted_iota(jnp.int32, (LANES, LANES), 0)
        c_i = lax.broadcasted_iota(jnp.int32, (LANES, LANES), 1)
        lower = (r_i > c_i).astype(_MXU_DTYPE)
        carry = jnp.zeros((1, QB), jnp.float32)
        for c in range(sv // LANES):
            sl = slice(c * LANES, (c + 1) * LANES)
            sc = score_s[sl, :]
            eq_c = sc == thr
            eq_f = eq_c.astype(jnp.float32)
            before = _dot(lower, eq_f.astype(_MXU_DTYPE)) + carry
            take = ((sc > thr) | (eq_c & (before < need_f))) & (sc >= _F32_LOWEST)
            bias_s[sl, :] = jnp.where(take, 0.0, -jnp.inf)
            carry = carry + jnp.sum(eq_f, axis=0, keepdims=True)

    c_exp = (ATT_HD ** -0.5) * float(np.log2(np.e))
    for h in range(ATT_HEADS):
        s = _dot_nt(k_s[0:sv, :], aq_ref[:, h * ATT_HD:(h + 1) * ATT_HD]) + bias_s[0:sv, :]
        m = jnp.max(s, axis=0, keepdims=True)
        p = jnp.exp2((s - m) * c_exp)
        l = jnp.sum(p, axis=0, keepdims=True)
        o_t = _dot(vt_s[:, 0:sv], p.astype(_MXU_DTYPE)) / l
        o_ref[:, h * ATT_HD:(h + 1) * ATT_HD] = o_t.T.astype(o_ref.dtype)


def _dsa_kernel(aq_ref, iq_ref, eq_ref, ek_ref, o_ref, k_s, ik_s, vt_s, score_s, bias_s, *, topk):
    S = ek_ref.shape[0]
    QB = aq_ref.shape[0]
    j = pl.program_id(1)

    @pl.when(j == 0)
    def _():
        k_s[...] = ek_ref[:, 0:LANES].astype(k_s.dtype)
        vt_s[...] = ek_ref[:, LANES:2 * LANES].T.astype(vt_s.dtype)
        ik_s[...] = ek_ref[:, 2 * LANES:3 * LANES].astype(ik_s.dtype)

    n_blk = S // QB
    n_cls = min(_DSA_CLASSES, n_blk)
    per_class = n_blk // n_cls
    for c in range(n_cls):
        @pl.when((j >= c * per_class) & (j < (c + 1) * per_class))
        def _(c=c):
            _dsa_block((c + 1) * per_class * QB, j, aq_ref, iq_ref, eq_ref, o_ref,
                       k_s, ik_s, vt_s, score_s, bias_s, topk)


def _dsa(aq, iq, e, batch, seq):
    T = aq.shape[0]
    nb = seq // QBLK
    topk = min(TOPK, seq // 4)
    qrow = lambda b, j: (b * nb + j, 0)
    return pl.pallas_call(
        functools.partial(_dsa_kernel, topk=topk),
        grid=(batch, nb),
        in_specs=[pl.BlockSpec((QBLK, ATT_HEADS * ATT_HD), qrow),
                  pl.BlockSpec((QBLK, IDX_HEADS * LANES), qrow),
                  pl.BlockSpec((QBLK, 4 * LANES), qrow),
                  pl.BlockSpec((seq, 4 * LANES), lambda b, j: (b, 0))],
        out_specs=pl.BlockSpec((QBLK, ATT_HEADS * ATT_HD), qrow),
        out_shape=jax.ShapeDtypeStruct((T, ATT_HEADS * ATT_HD), aq.dtype),
        scratch_shapes=[pltpu.VMEM((seq, LANES), aq.dtype),
                        pltpu.VMEM((seq, LANES), aq.dtype),
                        pltpu.VMEM((LANES, seq), aq.dtype),
                        pltpu.VMEM((seq, QBLK), jnp.float32),
                        pltpu.VMEM((seq, QBLK), jnp.float32)],
        compiler_params=_cparams(("parallel", "arbitrary")),
        name="dsa",
    )(aq, iq, e, e)


def _layer_norm(z, g, b):
    mu = jnp.mean(z, axis=-1, keepdims=True)
    zc = z - mu
    var = jnp.mean(zc * zc, axis=-1, keepdims=True)
    return zc * lax.rsqrt(var + LN_EPS_) * g + b


def _split_hi_lo(a):
    hi = a.astype(_MXU_DTYPE)
    lo = (a - hi.astype(jnp.float32)).astype(_MXU_DTYPE)
    return hi, lo


def _merge_kernel(ret_ref, att_ref, gr_ref, ga_ref, x_ref, wr_ref, wd_ref, wo_ref, g_ref, b_ref,
                  rwh_ref, rwl_ref, rb_ref, h_ref, route_ref, cnt_ref, carry_s):
    @pl.when(pl.program_id(0) == 0)
    def _():
        carry_s[...] = jnp.zeros_like(carry_s)

    a = _dot(ret_ref[...], wr_ref[...])
    d = _dot(att_ref[...], wd_ref[...])
    merged = (jax.nn.sigmoid(gr_ref[...].astype(jnp.float32)) * a
              + jax.nn.sigmoid(ga_ref[...].astype(jnp.float32)) * d)
    mix = _dot(merged.astype(_MXU_DTYPE), wo_ref[...])
    h = _layer_norm(ALPHA * x_ref[...] + mix, g_ref[...], b_ref[...])
    _store_tiled(h_ref, h)
    h_hi, h_lo = _split_hi_lo(h)
    logits = (_dot(h_hi, rwh_ref[...]) + _dot(h_lo, rwh_ref[...]) + _dot(h_hi, rwl_ref[...])
              + rb_ref[...])
    route_ref[...] = _route_tile(logits, carry_s)
    cnt_ref[...] = carry_s[...]


def _merge(ret, att, gates, x, wr, wd, wo, g, b, rw, rb, tm=512):
    T, D = x.shape
    row = lambda i: (i, 0)
    const = lambda i: (0, 0)
    rw_hi, rw_lo = _split_hi_lo(rw)
    return pl.pallas_call(
        _merge_kernel,
        grid=(T // tm,),
        in_specs=[pl.BlockSpec((tm, ret.shape[1]), row),
                  pl.BlockSpec((tm, att.shape[1]), row),
                  pl.BlockSpec((tm, D), lambda i: (i, 0)),
                  pl.BlockSpec((tm, D), lambda i: (i, 1)),
                  pl.BlockSpec((tm, D), row),
                  pl.BlockSpec(wr.shape, const),
                  pl.BlockSpec(wd.shape, const),
                  pl.BlockSpec(wo.shape, const),
                  pl.BlockSpec((1, D), const),
                  pl.BlockSpec((1, D), const),
                  pl.BlockSpec(rw.shape, const),
                  pl.BlockSpec(rw.shape, const),
                  pl.BlockSpec((1, LANES), const)],
        out_specs=[pl.BlockSpec((tm * ROW_TILES, LANES), row), pl.BlockSpec((tm, LANES), row),
                   pl.BlockSpec((SUBLANES, LANES), const)],
        out_shape=[jax.ShapeDtypeStruct((T * ROW_TILES, LANES), jnp.float32),
                   jax.ShapeDtypeStruct((T, LANES), jnp.float32),
                   jax.ShapeDtypeStruct((SUBLANES, LANES), jnp.float32)],
        scratch_shapes=[pltpu.VMEM((SUBLANES, LANES), jnp.float32)],
        compiler_params=_cparams(("arbitrary",)),
        name="merge_ln_router",
    )(ret, att, gates, gates, x, wr, wd, wo, g, b, rw_hi, rw_lo, rb)


def _route_tile(lg, carry_s):
    tm = lg.shape[0]
    lane = lax.broadcasted_iota(jnp.int32, (tm, LANES), 1).astype(jnp.float32)
    neg = -jnp.inf

    def first_max(v):
        m = jnp.max(v, axis=-1, keepdims=True)
        idx = jnp.min(jnp.where(v == m, lane, float(LANES)), axis=-1, keepdims=True)
        return m, idx

    gl = jnp.where(lane < N_GRP, lg, neg)
    gmax, gsel = first_max(gl)
    p_grp = 1.0 / jnp.sum(jnp.exp(gl - gmax), axis=-1, keepdims=True)
    lo = N_GRP + gsel * EXP_PER_GRP
    el = jnp.where((lane >= lo) & (lane < lo + EXP_PER_GRP), lg, neg)
    top1, i1 = first_max(el)
    top2, i2 = first_max(jnp.where(lane == i1, neg, el))
    z = jnp.exp(top2 - top1)
    g1 = p_grp / (1.0 + z)
    g2 = p_grp * z / (1.0 + z)
    oh1 = lane == i1
    oh2 = lane == i2
    oh = (oh1 | oh2).astype(jnp.float32)
    r_i = lax.broadcasted_iota(jnp.int32, (tm, tm), 0)
    c_i = lax.broadcasted_iota(jnp.int32, (tm, tm), 1)
    lower = (r_i > c_i).astype(_MXU_DTYPE)
    before = _dot(lower, oh.astype(_MXU_DTYPE)) + carry_s[0:1, :]
    rank1 = jnp.sum(jnp.where(oh1, before, 0.0), axis=-1, keepdims=True)
    rank2 = jnp.sum(jnp.where(oh2, before, 0.0), axis=-1, keepdims=True)
    new_carry = carry_s[0:1, :] + jnp.sum(oh, axis=0, keepdims=True)
    carry_s[...] = jnp.broadcast_to(new_carry, carry_s.shape)
    e1 = i1 - N_GRP
    e2 = i2 - N_GRP
    out = jnp.zeros((tm, LANES), jnp.float32)
    for col, val in enumerate((e1, e2, rank1, rank2, g1, g2)):
        out = jnp.where(lane == col, val, out)
    return out


_GATHER_UNROLL = 8
_GATHER_SEMS = (1, 8)


def _token_copy(src_hbm, dst_vmem, sem, src_tok, dst_tok):
    src = src_hbm.at[pl.ds(pl.multiple_of(src_tok * ROW_TILES, ROW_TILES), ROW_TILES)]
    dst = dst_vmem.at[pl.ds(pl.multiple_of(dst_tok * ROW_TILES, ROW_TILES), ROW_TILES)]
    return pltpu.make_async_copy(src, dst, sem)


def _start_token_gather(idx_ref, src_hbm, dst_vmem, sems, n_sem, n, stride=1, offset=0):
    def issue(q, c):
        for k in range(n_sem):
            r = q * n_sem + k
            _token_copy(src_hbm, dst_vmem, sems.at[k], idx_ref[0, 0, offset + r * stride], r).start()
        return c

    lax.fori_loop(0, n // n_sem, issue, 0, unroll=max(1, _GATHER_UNROLL // n_sem))


def _wait_token_gather(src_hbm, dst_vmem, sems, n_sem, n):
    rows = n // n_sem * ROW_TILES
    for k in range(n_sem):
        pltpu.make_async_copy(src_hbm.at[pl.ds(0, rows)], dst_vmem.at[pl.ds(0, rows)],
                              sems.at[k]).wait()


def _ffn_kernel(be_ref, tok_ref, tok_next_ref, h_hbm, wg_ref, wu_ref, wd_ref, y_ref,
                x_buf, wg_s, wu_s, wd_s, sem, *, n_sem):
    i = pl.program_id(0)
    n = pl.num_programs(0)
    slot = i % 2

    @pl.when(i == 0)
    def _():
        _start_token_gather(tok_ref, h_hbm, x_buf.at[0], sem.at[0], n_sem, FFN_BLK)

    @pl.when(i + 1 < n)
    def _():
        _start_token_gather(tok_next_ref, h_hbm, x_buf.at[1 - slot], sem.at[1 - slot], n_sem,
                            FFN_BLK)

    @pl.when((i == 0) | (be_ref[i] != be_ref[jnp.maximum(i - 1, 0)]))
    def _():
        wg_s[...] = wg_ref[0].astype(wg_s.dtype)
        wu_s[...] = wu_ref[0].astype(wu_s.dtype)
        wd_s[...] = wd_ref[0].astype(wd_s.dtype)

    _wait_token_gather(h_hbm, x_buf.at[slot], sem.at[slot], n_sem, FFN_BLK)
    xb = _load_tiled(x_buf.at[slot], FFN_BLK).astype(_MXU_DTYPE)
    gate = _dot(xb, wg_s[...])
    up = _dot(xb, wu_s[...])
    act = (gate * jax.nn.sigmoid(gate)) * up
    _store_tiled(y_ref, _dot(act.astype(_MXU_DTYPE), wd_s[...]))


def _ffn(block_expert, tok_pad, h, wg, wu, wd, layer, n_sem):
    n_blocks = block_expert.shape[0]
    D, F = wg.shape[2], wg.shape[3]
    wmap = lambda i, be: (layer, be[i], 0, 0)
    tok3 = tok_pad.reshape(n_blocks, 1, FFN_BLK)
    blk_rows = FFN_BLK * ROW_TILES
    grid_spec = pltpu.PrefetchScalarGridSpec(
        num_scalar_prefetch=1,
        grid=(n_blocks,),
        in_specs=[pl.BlockSpec((1, 1, FFN_BLK), lambda i, be: (i, 0, 0), memory_space=pltpu.SMEM),
                  pl.BlockSpec((1, 1, FFN_BLK), lambda i, be: (jnp.minimum(i + 1, n_blocks - 1), 0, 0),
                               memory_space=pltpu.SMEM),
                  pl.BlockSpec(memory_space=pl.ANY),
                  pl.BlockSpec((None, 1, D, F), wmap),
                  pl.BlockSpec((None, 1, D, F), wmap),
                  pl.BlockSpec((None, 1, F, D), wmap)],
        out_specs=pl.BlockSpec((blk_rows, LANES), lambda i, be: (i, 0)),
        scratch_shapes=[pltpu.VMEM((2, blk_rows, LANES), jnp.float32),
                        pltpu.VMEM((D, F), _MXU_DTYPE),
                        pltpu.VMEM((D, F), _MXU_DTYPE),
                        pltpu.VMEM((F, D), _MXU_DTYPE),
                        pltpu.SemaphoreType.DMA((2, n_sem))],
    )
    return pl.pallas_call(
        functools.partial(_ffn_kernel, n_sem=n_sem),
        grid_spec=grid_spec,
        out_shape=jax.ShapeDtypeStruct((n_blocks * blk_rows, LANES), jnp.float32),
        compiler_params=_cparams(("arbitrary",)),
        name="expert_ffn",
    )(block_expert, tok3, tok3, h, wg, wu, wd)


def _combine_kernel(dest_ref, dest_next_ref, y_hbm, r_ref, h_ref, g_ref, b_ref, o_ref, ob_ref,
                    y_buf, sem, *, n_sem):
    tm = o_ref.shape[0]
    i = pl.program_id(0)
    n = pl.num_programs(0)
    slot = i % 2

    def start(idx_ref, s):
        for c in range(2):
            _start_token_gather(idx_ref, y_hbm, y_buf.at[s, c], sem.at[s, c], n_sem, tm, stride=2,
                                offset=c)

    @pl.when(i == 0)
    def _():
        start(dest_ref, 0)

    @pl.when(i + 1 < n)
    def _():
        start(dest_next_ref, 1 - slot)

    for c in range(2):
        _wait_token_gather(y_hbm, y_buf.at[slot, c], sem.at[slot, c], n_sem, tm)
    r = r_ref[...]
    g0 = r[:, 4:5]
    g1 = r[:, 5:6]
    z = jnp.concatenate(
        [ALPHA * _tile_rows(h_ref, tm, s)[...]
         + _tile_rows(y_buf.at[slot, 0], tm, s)[...] * g0
         + _tile_rows(y_buf.at[slot, 1], tm, s)[...] * g1 for s in range(ROW_TILES)], axis=1)
    out = _layer_norm(z, g_ref[...], b_ref[...])
    o_ref[...] = out
    ob_ref[...] = out.astype(ob_ref.dtype)


def _combine(dest, y_pad, route, h, g, b, n_sem, tm=128):
    T = route.shape[0]
    D = g.shape[1]
    n = T // tm
    row = lambda i: (i, 0)
    const = lambda i: (0, 0)
    dest3 = dest.reshape(n, 1, 2 * tm)
    return pl.pallas_call(
        functools.partial(_combine_kernel, n_sem=n_sem),
        grid=(n,),
        in_specs=[pl.BlockSpec((1, 1, 2 * tm), lambda i: (i, 0, 0), memory_space=pltpu.SMEM),
                  pl.BlockSpec((1, 1, 2 * tm), lambda i: (jnp.minimum(i + 1, n - 1), 0, 0),
                               memory_space=pltpu.SMEM),
                  pl.BlockSpec(memory_space=pl.ANY),
                  pl.BlockSpec((tm, LANES), row),
                  pl.BlockSpec((tm * ROW_TILES, LANES), row),
                  pl.BlockSpec((1, D), const),
                  pl.BlockSpec((1, D), const)],
        out_specs=[pl.BlockSpec((tm, D), row), pl.BlockSpec((tm, D), row)],
        out_shape=[jax.ShapeDtypeStruct((T, D), jnp.float32),
                   jax.ShapeDtypeStruct((T, D), _MXU_DTYPE)],
        scratch_shapes=[pltpu.VMEM((2, 2, tm * ROW_TILES, LANES), jnp.float32),
                        pltpu.SemaphoreType.DMA((2, 2, n_sem))],
        compiler_params=_cparams(("arbitrary",)),
        name="combine_ln",
    )(dest3, dest3, y_pad, route, h, g, b)


def _spread64(wt):
    n, K = wt.shape[0] // IDX_D, wt.shape[1]
    wt = wt.reshape(n, 2, IDX_D // 2, K)
    wt = jnp.pad(wt, ((0, 0), (0, 0), (0, IDX_D // 2), (0, 0)))
    return wt.reshape(n * LANES, K)


_C_RQ = 0
_C_RV = 2 * RET_HEADS * RET_DK
_C_AQ = _C_RV + 2 * RET_HEADS * RET_DV
_C_AK = _C_AQ + ATT_HEADS * ATT_HD
_C_IQ = _C_AK + 2 * ATT_HD
_C_IK = _C_IQ + IDX_HEADS * IDX_D
_C_IW = _C_IK + IDX_D
_C_GR = _C_IW + IDX_HEADS
_C_END = _C_GR + 2 * D_MODEL_


def _moe_plan(route, counts):
    T = route.shape[0]
    n_assign = T * 2
    n_blocks = -(-n_assign // FFN_BLK) + N_EXP
    cnt = counts[0, N_GRP:N_GRP + N_EXP].astype(jnp.int32)
    padded = (cnt + FFN_BLK - 1) // FFN_BLK * FFN_BLK
    pad_ends = jnp.cumsum(padded)
    pad_starts = pad_ends - padded
    expert = route[:, 0:2].astype(jnp.int32)
    rank = route[:, 2:4].astype(jnp.int32)
    is_e = expert[:, :, None] == jnp.arange(N_EXP, dtype=jnp.int32)
    dest = (jnp.sum(jnp.where(is_e, pad_starts, 0), axis=-1) + rank).reshape(n_assign)
    tok = jnp.repeat(jnp.arange(T, dtype=jnp.int32), 2)
    tok_pad = jnp.zeros((n_blocks * FFN_BLK,), jnp.int32).at[dest].set(tok)
    block_start = jnp.arange(n_blocks, dtype=jnp.int32) * FFN_BLK
    block_expert = jnp.minimum(
        jnp.sum((pad_ends[None, :] <= block_start[:, None]).astype(jnp.int32), axis=1), N_EXP - 1)
    return dest, tok_pad, block_expert


def kernel(x, w_in, w_ret_o, w_dsa_o, w_out, ln1_g, ln1_b, router_grp_w, router_grp_b,
           router_exp_w, router_exp_b, w_gate, w_up, w_down, ln2_g, ln2_b):
    B, S, D = x.shape
    T = B * S
    cdt = _MXU_DTYPE
    assert w_in.shape[-1] == _C_END
    t_ret = [jnp.asarray(t) for t in _rot_tables(S, RET_DK, 4 * RET_DK, "half")]
    t_att = [jnp.asarray(t) for t in _rot_tables(S, ATT_HD, 8 * ATT_HD, "half")]
    t_idx = [jnp.asarray(t) for t in _rot_tables(S, IDX_D, 8 * LANES, "spread")]
    ident = (np.ones((S, LANES), np.float32), np.zeros((S, LANES), np.float32))
    t_e = [jnp.asarray(np.concatenate([ta[:, :LANES], i, ti[:, :LANES], i], axis=1))
           for ta, ti, i in zip(_rot_tables(S, ATT_HD, LANES, "half"),
                                _rot_tables(S, IDX_D, LANES, "spread"), ident)]
    xf = x.reshape(T, D)
    xb = xf.astype(cdt)
    w_in_t = jnp.swapaxes(w_in, 1, 2)
    for l in range(DEPTH_):
        w_iq = _spread64(w_in_t[l, _C_IQ:_C_IK])
        w_e = jnp.concatenate([w_in_t[l, _C_AK:_C_IQ], _spread64(w_in_t[l, _C_IK:_C_IW]),
                               jnp.pad(w_in_t[l, _C_IW:_C_GR], ((0, LANES - IDX_HEADS), (0, 0)))],
                              axis=0)
        w_gates = w_in_t[l, _C_GR:_C_END]
        qk = _proj(xb, w_in_t, B, S, layer=l, col0=_C_RQ, n_cols=_C_RV - _C_RQ, rot=RET_DK,
                   tables=t_ret, tm=512)
        vg = _proj(xb, w_in_t, B, S, layer=l, col0=_C_RV, n_cols=_C_AQ - _C_RV)
        aq = _proj(xb, w_in_t, B, S, layer=l, col0=_C_AQ, n_cols=_C_AK - _C_AQ, rot=ATT_HD,
                   tables=t_att, tm=512)
        iq = _proj(xb, w_iq, B, S, rot=LANES, tables=t_idx, tm=512)
        e = _proj(xb, w_e, B, S, rot=LANES, tables=t_e, out_dtype=jnp.float32, tm=512)
        gates = _proj(xb, w_gates, B, S)
        ret = _retention(qk, vg, B, S)
        att = _dsa(aq, iq, e, B, S)
        rw = jnp.pad(jnp.concatenate([router_grp_w[l], router_exp_w[l]], axis=1),
                     ((0, 0), (0, LANES - N_GRP - N_EXP)))
        rb = jnp.pad(jnp.concatenate([router_grp_b[l], router_exp_b[l]]),
                     (0, LANES - N_GRP - N_EXP)).reshape(1, LANES)
        h, route, counts = _merge(ret, att, gates, xf, w_ret_o[l].astype(cdt),
                                  w_dsa_o[l].astype(cdt), w_out[l].astype(cdt),
                                  ln1_g[l].reshape(1, D), ln1_b[l].reshape(1, D), rw, rb)
        dest, tok_pad, block_expert = _moe_plan(route, counts)
        n_sem = _GATHER_SEMS[l]
        y_pad = _ffn(block_expert, tok_pad, h, w_gate, w_up, w_down, l, n_sem)
        xf, xb = _combine(dest, y_pad, route, h, ln2_g[l].reshape(1, D), ln2_b[l].reshape(1, D),
                          n_sem)
    return xf.reshape(B, S, D)
```

```python
import functools

import numpy as np
import jax
import jax.numpy as jnp
from jax import lax
from jax.experimental import pallas as pl
from jax.experimental.pallas import tpu as pltpu

D_MODEL_ = 1024
DEPTH_ = 2
RET_HEADS = 4
RET_DK = 256
RET_DV = 512
RET_C = 128
ATT_HEADS = 8
ATT_HD = 128
IDX_HEADS = 8
IDX_D = 64
TOPK = 256
QBLK = 128
N_GRP = 4
EXP_PER_GRP = 8
N_EXP = N_GRP * EXP_PER_GRP
EXP_FF = 512
FFN_BLK = 128
THETA = 10000.0
LN_EPS_ = 1e-5
GN_EPS_ = 1e-5
ALPHA = (2 * DEPTH_) ** 0.25

LANES = 128
SUBLANES = 8
INT_MIN = -(2 ** 31)
_CODE_NEG_INF = INT_MIN + 0x7FFFFF
_F32_LOWEST = float(np.finfo(np.float32).min)

_MXU_DTYPE = jnp.bfloat16
_VMEM_LIMIT = 56 * 1024 * 1024

ROW_TILES = D_MODEL_ // LANES
assert ROW_TILES == SUBLANES


def _cparams(sem):
    return pltpu.CompilerParams(dimension_semantics=sem, vmem_limit_bytes=_VMEM_LIMIT)


def _dot(a, b):
    return jnp.dot(a, b, preferred_element_type=jnp.float32)


def _dot_nt(a, b):
    return lax.dot_general(a, b, (((1,), (1,)), ((), ())), preferred_element_type=jnp.float32)


def _dot_tn(a, b):
    return lax.dot_general(a, b, (((0,), (0,)), ((), ())), preferred_element_type=jnp.float32)


def _tile_rows(ref, n, s):
    return ref.at[pl.ds(s, n, stride=ROW_TILES), :]


def _load_tiled(ref, n):
    return jnp.concatenate([_tile_rows(ref, n, s)[...] for s in range(ROW_TILES)], axis=1)


def _store_tiled(ref, val):
    n = val.shape[0]
    for s in range(ROW_TILES):
        _tile_rows(ref, n, s)[...] = val[:, s * LANES:(s + 1) * LANES]


def _swap_halves(a, d):
    tn = a.shape[1]
    if d == 2 * LANES:
        parts = []
        for c in range(0, tn, d):
            parts += [a[:, c + LANES:c + d], a[:, c:c + LANES]]
        return jnp.concatenate(parts, axis=1)
    assert d == LANES
    return jnp.concatenate(
        [pltpu.roll(a[:, c:c + LANES], LANES // 2, axis=1) for c in range(0, tn, LANES)], axis=1)


def _proj_kernel(x_ref, w_ref, *rest, rot):
    if rot:
        c_ref, s_ref, o_ref, w_s = rest
    else:
        o_ref, w_s = rest

    @pl.when((pl.program_id(1) == 0) & (pl.program_id(2) == 0))
    def _():
        w_s[...] = w_ref[...].astype(w_s.dtype)

    acc = _dot_nt(x_ref[...], w_s[...])
    if rot:
        acc = acc * c_ref[...] + _swap_halves(acc, rot) * s_ref[...]
    o_ref[...] = acc.astype(o_ref.dtype)


def _proj(x, wt, batch, seq, *, layer=None, col0=0, n_cols=None, rot=0, tables=None,
          out_dtype=None, tm=1024, tn=1024):
    T, K = x.shape
    n_cols = n_cols or wt.shape[-2]
    tn = min(tn, n_cols)
    tm = min(tm, seq)
    n_seq = seq // tm
    assert col0 % tn == 0 and n_cols % tn == 0 and seq % tm == 0
    c0 = col0 // tn
    row = lambda j, p, b: (b * n_seq + p, 0)
    if layer is None:
        w_spec = pl.BlockSpec((tn, K), lambda j, p, b: (c0 + j, 0))
    else:
        w_spec = pl.BlockSpec((None, tn, K), lambda j, p, b: (layer, c0 + j, 0))
    in_specs = [pl.BlockSpec((tm, K), row), w_spec]
    args = [x, wt]
    if rot:
        in_specs += [pl.BlockSpec((tm, tn), lambda j, p, b: (p, 0))] * 2
        args += list(tables)
    return pl.pallas_call(
        functools.partial(_proj_kernel, rot=rot),
        grid=(n_cols // tn, n_seq, batch),
        in_specs=in_specs,
        out_specs=pl.BlockSpec((tm, tn), lambda j, p, b: (b * n_seq + p, j)),
        out_shape=jax.ShapeDtypeStruct((T, n_cols), out_dtype or x.dtype),
        scratch_shapes=[pltpu.VMEM((tn, K), _MXU_DTYPE)],
        compiler_params=_cparams(("parallel", "arbitrary", "arbitrary")),
        name="proj_rot%d" % rot,
    )(*args)


def _rot_tables(seq, d, width, layout):
    inv = 1.0 / (THETA ** (np.arange(0, d, 2, dtype=np.float64) / d))
    ang = np.arange(seq, dtype=np.float64)[:, None] * inv[None, :]
    cos, sin = np.cos(ang), np.sin(ang)
    if layout == "half":
        c = np.concatenate([cos, cos], axis=1)
        s = np.concatenate([-sin, sin], axis=1)
    else:
        z = np.zeros_like(cos)
        c = np.concatenate([cos, z, cos, z], axis=1)
        s = np.concatenate([-sin, z, sin, z], axis=1)
    reps = width // c.shape[1]
    return np.tile(c, (1, reps)).astype(np.float32), np.tile(s, (1, reps)).astype(np.float32)


_RET_CHUNKS_PER_STEP = 4


def _ret_kernel(qk_ref, v_ref, g_ref, dec_ref, xi_ref, zeta_ref, o_ref, r_ref, *, gamma_c):
    H, DK, DV, C = RET_HEADS, RET_DK, RET_DV, RET_C

    @pl.when(pl.program_id(1) == 0)
    def _():
        r_ref[...] = jnp.zeros_like(r_ref)

    for h in range(H):
        xi = jnp.concatenate([xi_ref[h]] * (DV // LANES), axis=1)
        zeta = jnp.concatenate([zeta_ref[h]] * (DV // LANES), axis=1)
        for c in range(qk_ref.shape[0] // C):
            rows = slice(c * C, (c + 1) * C)
            q = qk_ref[rows, h * DK:(h + 1) * DK]
            k = qk_ref[rows, (H + h) * DK:(H + h + 1) * DK]
            v = v_ref[rows, h * DV:(h + 1) * DV]
            s = _dot_nt(q, k) * dec_ref[h]
            intra = _dot(s.astype(_MXU_DTYPE), v)
            r_old = r_ref[h]
            cross = _dot(q, r_old.astype(_MXU_DTYPE))
            y = intra + cross * xi
            vz = (v.astype(jnp.float32) * zeta).astype(_MXU_DTYPE)
            r_ref[h] = gamma_c[h] * r_old + _dot_tn(k, vz)
            mu = jnp.mean(y, axis=-1, keepdims=True)
            yc = y - mu
            var = jnp.mean(yc * yc, axis=-1, keepdims=True)
            yn = yc * lax.rsqrt(var + GN_EPS_)
            g = g_ref[rows, h * DV:(h + 1) * DV].astype(jnp.float32)
            o_ref[rows, h * DV:(h + 1) * DV] = (yn * (g * jax.nn.sigmoid(g))).astype(o_ref.dtype)


def _retention(qk, vg, batch, seq):
    H, DK, DV, C = RET_HEADS, RET_DK, RET_DV, RET_C
    rows = C * _RET_CHUNKS_PER_STEP
    n_steps = seq // rows
    lg = np.log(1.0 - 2.0 ** (-5.0 - np.arange(H, dtype=np.float64)))
    pos = np.arange(C, dtype=np.float64)
    diff = pos[:, None] - pos[None, :]
    dec = np.where(diff >= 0, np.exp(np.maximum(diff, 0.0)[None] * lg[:, None, None]), 0.0)
    dec = (dec * DK ** -0.5).astype(np.float32)
    ones = np.ones((1, 1, LANES))
    xi = (np.exp((pos[None, :] + 1.0) * lg[:, None])[:, :, None] * ones).astype(np.float32)
    zeta = (np.exp((C - 1.0 - pos)[None, :] * lg[:, None])[:, :, None] * ones
            * DK ** -0.5).astype(np.float32)
    gamma_c = tuple(float(g) for g in np.exp(C * lg))
    T = qk.shape[0]
    const3 = lambda b, n: (0, 0, 0)
    row = lambda b, n: (b * n_steps + n, 0)
    return pl.pallas_call(
        functools.partial(_ret_kernel, gamma_c=gamma_c),
        grid=(batch, n_steps),
        in_specs=[pl.BlockSpec((rows, 2 * H * DK), row),
                  pl.BlockSpec((rows, H * DV), row),
                  pl.BlockSpec((rows, H * DV), lambda b, n: (b * n_steps + n, 1)),
                  pl.BlockSpec((H, C, C), const3),
                  pl.BlockSpec((H, C, LANES), const3),
                  pl.BlockSpec((H, C, LANES), const3)],
        out_specs=pl.BlockSpec((rows, H * DV), row),
        out_shape=jax.ShapeDtypeStruct((T, H * DV), qk.dtype),
        scratch_shapes=[pltpu.VMEM((H, DK, DV), jnp.float32)],
        compiler_params=_cparams(("parallel", "arbitrary")),
        name="retention",
    )(qk, vg, vg, jnp.asarray(dec), jnp.asarray(xi), jnp.asarray(zeta))


_COUNT_CHAINS = 8
_DSA_CLASSES = 8


def _col_count(mask):
    n, qb = mask.shape
    part = jnp.sum(mask.astype(jnp.int32).reshape(_COUNT_CHAINS, n // _COUNT_CHAINS, qb), axis=1)
    return jnp.sum(part, axis=0, keepdims=True)


def _dsa_block(sv, j, aq_ref, iq_ref, eq_ref, o_ref, k_s, ik_s, vt_s, score_s, bias_s, topk):
    QB = aq_ref.shape[0]
    w_t = eq_ref[:, 3 * LANES:4 * LANES].T * ((IDX_D ** -0.5) * (IDX_HEADS ** -0.5))
    score = jnp.zeros((sv, QB), jnp.float32)
    for h in range(IDX_HEADS):
        rel = jnp.maximum(_dot_nt(ik_s[0:sv, :], iq_ref[:, h * LANES:(h + 1) * LANES]), 0.0)
        score = score + rel * w_t[h:h + 1, :]
    key_pos = lax.broadcasted_iota(jnp.int32, (sv, QB), 0)
    q_pos = j * QB + lax.broadcasted_iota(jnp.int32, (sv, QB), 1)
    score_s[0:sv, :] = jnp.where(key_pos <= q_pos, score, -jnp.inf)

    def count_ge(c):
        return _col_count(score_s[0:sv, :] >= c)

    def float_of(code):
        return pltpu.bitcast(jnp.where(code < 0, code ^ 0x7FFFFFFF, code), jnp.float32)

    code = jnp.where(count_ge(jnp.zeros((1, QB), jnp.float32)) >= topk, 0, INT_MIN)

    def bit_step(i, code):
        cand = code | lax.shift_left(jnp.int32(1), 30 - i)
        ok = (count_ge(float_of(cand)) >= topk) | (cand <= _CODE_NEG_INF)
        return jnp.where(ok, cand, code)

    code = lax.fori_loop(0, 31, bit_step, code)
    thr = float_of(code)
    thr_lo = jnp.maximum(thr, _F32_LOWEST)
    has_tie = jnp.max(jnp.where(code > _CODE_NEG_INF, count_ge(thr) - topk, 0)) > 0

    @pl.when(jnp.logical_not(has_tie))
    def _():
        bias_s[0:sv, :] = jnp.where(score_s[0:sv, :] >= thr_lo, 0.0, -jnp.inf)

    @pl.when(has_tie)
    def _():
        need_f = (topk - _col_count(score_s[0:sv, :] > thr)).astype(jnp.float32)
        r_i = lax.broadcasted_iota(jnp.int32, (LANES, LANES), 0)
        c_i = lax.broadcasted_iota(jnp.int32, (LANES, LANES), 1)
        lower = (r_i > c_i).astype(_MXU_DTYPE)
        carry = jnp.zeros((1, QB), jnp.float32)
        for c in range(sv // LANES):
            sl = slice(c * LANES, (c + 1) * LANES)
            sc = score_s[sl, :]
            eq_c = sc == thr
            eq_f = eq_c.astype(jnp.float32)
            before = _dot(lower, eq_f.astype(_MXU_DTYPE)) + carry
            take = ((sc > thr) | (eq_c & (before < need_f))) & (sc >= _F32_LOWEST)
            bias_s[sl, :] = jnp.where(take, 0.0, -jnp.inf)
            carry = carry + jnp.sum(eq_f, axis=0, keepdims=True)

    c_exp = (ATT_HD ** -0.5) * float(np.log2(np.e))
    for h in range(ATT_HEADS):
        s = _dot_nt(k_s[0:sv, :], aq_ref[:, h * ATT_HD:(h + 1) * ATT_HD]) + bias_s[0:sv, :]
        m = jnp.max(s, axis=0, keepdims=True)
        p = jnp.exp2((s - m) * c_exp)
        l = jnp.sum(p, axis=0, keepdims=True)
        o_t = _dot(vt_s[:, 0:sv], p.astype(_MXU_DTYPE)) / l
        o_ref[:, h * ATT_HD:(h + 1) * ATT_HD] = o_t.T.astype(o_ref.dtype)


def _dsa_kernel(aq_ref, iq_ref, eq_ref, ek_ref, o_ref, k_s, ik_s, vt_s, score_s, bias_s, *, topk):
    S = ek_ref.shape[0]
    QB = aq_ref.shape[0]
    j = pl.program_id(1)

    @pl.when(j == 0)
    def _():
        k_s[...] = ek_ref[:, 0:LANES].astype(k_s.dtype)
        vt_s[...] = ek_ref[:, LANES:2 * LANES].T.astype(vt_s.dtype)
        ik_s[...] = ek_ref[:, 2 * LANES:3 * LANES].astype(ik_s.dtype)

    n_blk = S // QB
    n_cls = min(_DSA_CLASSES, n_blk)
    per_class = n_blk // n_cls
    for c in range(n_cls):
        @pl.when((j >= c * per_class) & (j < (c + 1) * per_class))
        def _(c=c):
            _dsa_block((c + 1) * per_class * QB, j, aq_ref, iq_ref, eq_ref, o_ref,
                       k_s, ik_s, vt_s, score_s, bias_s, topk)


def _dsa(aq, iq, e, batch, seq):
    T = aq.shape[0]
    nb = seq // QBLK
    topk = min(TOPK, seq // 4)
    qrow = lambda b, j: (b * nb + j, 0)
    return pl.pallas_call(
        functools.partial(_dsa_kernel, topk=topk),
        grid=(batch, nb),
        in_specs=[pl.BlockSpec((QBLK, ATT_HEADS * ATT_HD), qrow),
                  pl.BlockSpec((QBLK, IDX_HEADS * LANES), qrow),
                  pl.BlockSpec((QBLK, 4 * LANES), qrow),
                  pl.BlockSpec((seq, 4 * LANES), lambda b, j: (b, 0))],
        out_specs=pl.BlockSpec((QBLK, ATT_HEADS * ATT_HD), qrow),
        out_shape=jax.ShapeDtypeStruct((T, ATT_HEADS * ATT_HD), aq.dtype),
        scratch_shapes=[pltpu.VMEM((seq, LANES), aq.dtype),
                        pltpu.VMEM((seq, LANES), aq.dtype),
                        pltpu.VMEM((LANES, seq), aq.dtype),
                        pltpu.VMEM((seq, QBLK), jnp.float32),
                        pltpu.VMEM((seq, QBLK), jnp.float32)],
        compiler_params=_cparams(("parallel", "arbitrary")),
        name="dsa",
    )(aq, iq, e, e)


def _layer_norm(z, g, b):
    mu = jnp.mean(z, axis=-1, keepdims=True)
    zc = z - mu
    var = jnp.mean(zc * zc, axis=-1, keepdims=True)
    return zc * lax.rsqrt(var + LN_EPS_) * g + b


def _split_hi_lo(a):
    hi = a.astype(_MXU_DTYPE)
    lo = (a - hi.astype(jnp.float32)).astype(_MXU_DTYPE)
    return hi, lo


def _merge_kernel(ret_ref, att_ref, gr_ref, ga_ref, x_ref, wr_ref, wd_ref, wo_ref, g_ref, b_ref,
                  rwh_ref, rwl_ref, rb_ref, h_ref, route_ref, cnt_ref, carry_s):
    @pl.when(pl.program_id(0) == 0)
    def _():
        carry_s[...] = jnp.zeros_like(carry_s)

    a = _dot(ret_ref[...], wr_ref[...])
    d = _dot(att_ref[...], wd_ref[...])
    merged = (jax.nn.sigmoid(gr_ref[...].astype(jnp.float32)) * a
              + jax.nn.sigmoid(ga_ref[...].astype(jnp.float32)) * d)
    mix = _dot(merged.astype(_MXU_DTYPE), wo_ref[...])
    h = _layer_norm(ALPHA * x_ref[...] + mix, g_ref[...], b_ref[...])
    _store_tiled(h_ref, h)
    h_hi, h_lo = _split_hi_lo(h)
    logits = (_dot(h_hi, rwh_ref[...]) + _dot(h_lo, rwh_ref[...]) + _dot(h_hi, rwl_ref[...])
              + rb_ref[...])
    route_ref[...] = _route_tile(logits, carry_s)
    cnt_ref[...] = carry_s[...]


def _merge(ret, att, gates, x, wr, wd, wo, g, b, rw, rb, tm=512):
    T, D = x.shape
    row = lambda i: (i, 0)
    const = lambda i: (0, 0)
    rw_hi, rw_lo = _split_hi_lo(rw)
    return pl.pallas_call(
        _merge_kernel,
        grid=(T // tm,),
        in_specs=[pl.BlockSpec((tm, ret.shape[1]), row),
                  pl.BlockSpec((tm, att.shape[1]), row),
                  pl.BlockSpec((tm, D), lambda i: (i, 0)),
                  pl.BlockSpec((tm, D), lambda i: (i, 1)),
                  pl.BlockSpec((tm, D), row),
                  pl.BlockSpec(wr.shape, const),
                  pl.BlockSpec(wd.shape, const),
                  pl.BlockSpec(wo.shape, const),
                  pl.BlockSpec((1, D), const),
                  pl.BlockSpec((1, D), const),
                  pl.BlockSpec(rw.shape, const),
                  pl.BlockSpec(rw.shape, const),
                  pl.BlockSpec((1, LANES), const)],
        out_specs=[pl.BlockSpec((tm * ROW_TILES, LANES), row), pl.BlockSpec((tm, LANES), row),
                   pl.BlockSpec((SUBLANES, LANES), const)],
        out_shape=[jax.ShapeDtypeStruct((T * ROW_TILES, LANES), jnp.float32),
                   jax.ShapeDtypeStruct((T, LANES), jnp.float32),
                   jax.ShapeDtypeStruct((SUBLANES, LANES), jnp.float32)],
        scratch_shapes=[pltpu.VMEM((SUBLANES, LANES), jnp.float32)],
        compiler_params=_cparams(("arbitrary",)),
        name="merge_ln_router",
    )(ret, att, gates, gates, x, wr, wd, wo, g, b, rw_hi, rw_lo, rb)


def _route_tile(lg, carry_s):
    tm = lg.shape[0]
    lane = lax.broadcasted_iota(jnp.int32, (tm, LANES), 1).astype(jnp.float32)
    neg = -jnp.inf

    def first_max(v):
        m = jnp.max(v, axis=-1, keepdims=True)
        idx = jnp.min(jnp.where(v == m, lane, float(LANES)), axis=-1, keepdims=True)
        return m, idx

    gl = jnp.where(lane < N_GRP, lg, neg)
    gmax, gsel = first_max(gl)
    p_grp = 1.0 / jnp.sum(jnp.exp(gl - gmax), axis=-1, keepdims=True)
    lo = N_GRP + gsel * EXP_PER_GRP
    el = jnp.where((lane >= lo) & (lane < lo + EXP_PER_GRP), lg, neg)
    top1, i1 = first_max(el)
    top2, i2 = first_max(jnp.where(lane == i1, neg, el))
    z = jnp.exp(top2 - top1)
    g1 = p_grp / (1.0 + z)
    g2 = p_grp * z / (1.0 + z)
    oh1 = lane == i1
    oh2 = lane == i2
    oh = (oh1 | oh2).astype(jnp.float32)
    r_i = lax.broadcasted_iota(jnp.int32, (tm, tm), 0)
    c_i = lax.broadcasted_iota(jnp.int32, (tm, tm), 1)
    lower = (r_i > c_i).astype(_MXU_DTYPE)
    before = _dot(lower, oh.astype(_MXU_DTYPE)) + carry_s[0:1, :]
    rank1 = jnp.sum(jnp.where(oh1, before, 0.0), axis=-1, keepdims=True)
    rank2 = jnp.sum(jnp.where(oh2, before, 0.0), axis=-1, keepdims=True)
    new_carry = carry_s[0:1, :] + jnp.sum(oh, axis=0, keepdims=True)
    carry_s[...] = jnp.broadcast_to(new_carry, carry_s.shape)
    e1 = i1 - N_GRP
    e2 = i2 - N_GRP
    out = jnp.zeros((tm, LANES), jnp.float32)
    for col, val in enumerate((e1, e2, rank1, rank2, g1, g2)):
        out = jnp.where(lane == col, val, out)
    return out


_GATHER_UNROLL = 8


def _token_copy(src_hbm, dst_vmem, sem, src_tok, dst_tok):
    src = src_hbm.at[pl.ds(pl.multiple_of(src_tok * ROW_TILES, ROW_TILES), ROW_TILES)]
    dst = dst_vmem.at[pl.ds(pl.multiple_of(dst_tok * ROW_TILES, ROW_TILES), ROW_TILES)]
    return pltpu.make_async_copy(src, dst, sem)


def _start_token_gather(idx_ref, src_hbm, dst_vmem, sem, n, stride=1, offset=0):
    def issue(r, c):
        _token_copy(src_hbm, dst_vmem, sem, idx_ref[0, 0, offset + r * stride], r).start()
        return c

    lax.fori_loop(0, n, issue, 0, unroll=_GATHER_UNROLL)


def _wait_token_gather(src_hbm, dst_vmem, sem, n):
    pltpu.make_async_copy(src_hbm.at[pl.ds(0, n * ROW_TILES)], dst_vmem, sem).wait()


def _ffn_kernel(be_ref, tok_ref, tok_next_ref, h_hbm, wg_ref, wu_ref, wd_ref, y_ref,
                x_buf, wg_s, wu_s, wd_s, sem):
    i = pl.program_id(0)
    n = pl.num_programs(0)
    slot = i % 2

    @pl.when(i == 0)
    def _():
        _start_token_gather(tok_ref, h_hbm, x_buf.at[0], sem.at[0], FFN_BLK)

    @pl.when(i + 1 < n)
    def _():
        _start_token_gather(tok_next_ref, h_hbm, x_buf.at[1 - slot], sem.at[1 - slot], FFN_BLK)

    @pl.when((i == 0) | (be_ref[i] != be_ref[jnp.maximum(i - 1, 0)]))
    def _():
        wg_s[...] = wg_ref[0].astype(wg_s.dtype)
        wu_s[...] = wu_ref[0].astype(wu_s.dtype)
        wd_s[...] = wd_ref[0].astype(wd_s.dtype)

    _wait_token_gather(h_hbm, x_buf.at[slot], sem.at[slot], FFN_BLK)
    xb = _load_tiled(x_buf.at[slot], FFN_BLK).astype(_MXU_DTYPE)
    gate = _dot(xb, wg_s[...])
    up = _dot(xb, wu_s[...])
    act = (gate * jax.nn.sigmoid(gate)) * up
    _store_tiled(y_ref, _dot(act.astype(_MXU_DTYPE), wd_s[...]))


def _ffn(block_expert, tok_pad, h, wg, wu, wd, layer):
    n_blocks = block_expert.shape[0]
    D, F = wg.shape[2], wg.shape[3]
    wmap = lambda i, be: (layer, be[i], 0, 0)
    tok3 = tok_pad.reshape(n_blocks, 1, FFN_BLK)
    blk_rows = FFN_BLK * ROW_TILES
    grid_spec = pltpu.PrefetchScalarGridSpec(
        num_scalar_prefetch=1,
        grid=(n_blocks,),
        in_specs=[pl.BlockSpec((1, 1, FFN_BLK), lambda i, be: (i, 0, 0), memory_space=pltpu.SMEM),
                  pl.BlockSpec((1, 1, FFN_BLK), lambda i, be: (jnp.minimum(i + 1, n_blocks - 1), 0, 0),
                               memory_space=pltpu.SMEM),
                  pl.BlockSpec(memory_space=pl.ANY),
                  pl.BlockSpec((None, 1, D, F), wmap),
                  pl.BlockSpec((None, 1, D, F), wmap),
                  pl.BlockSpec((None, 1, F, D), wmap)],
        out_specs=pl.BlockSpec((blk_rows, LANES), lambda i, be: (i, 0)),
        scratch_shapes=[pltpu.VMEM((2, blk_rows, LANES), jnp.float32),
                        pltpu.VMEM((D, F), _MXU_DTYPE),
                        pltpu.VMEM((D, F), _MXU_DTYPE),
                        pltpu.VMEM((F, D), _MXU_DTYPE),
                        pltpu.SemaphoreType.DMA((2,))],
    )
    return pl.pallas_call(
        _ffn_kernel,
        grid_spec=grid_spec,
        out_shape=jax.ShapeDtypeStruct((n_blocks * blk_rows, LANES), jnp.float32),
        compiler_params=_cparams(("arbitrary",)),
        name="expert_ffn",
    )(block_expert, tok3, tok3, h, wg, wu, wd)


def _combine_kernel(dest_ref, dest_next_ref, y_hbm, r_ref, h_ref, g_ref, b_ref, o_ref, ob_ref,
                    y_buf, sem):
    tm = o_ref.shape[0]
    i = pl.program_id(0)
    n = pl.num_programs(0)
    slot = i % 2

    def start(idx_ref, s):
        for c in range(2):
            _start_token_gather(idx_ref, y_hbm, y_buf.at[s, c], sem.at[s, c], tm, stride=2, offset=c)

    @pl.when(i == 0)
    def _():
        start(dest_ref, 0)

    @pl.when(i + 1 < n)
    def _():
        start(dest_next_ref, 1 - slot)

    for c in range(2):
        _wait_token_gather(y_hbm, y_buf.at[slot, c], sem.at[slot, c], tm)
    r = r_ref[...]
    g0 = r[:, 4:5]
    g1 = r[:, 5:6]
    z = jnp.concatenate(
        [ALPHA * _tile_rows(h_ref, tm, s)[...]
         + _tile_rows(y_buf.at[slot, 0], tm, s)[...] * g0
         + _tile_rows(y_buf.at[slot, 1], tm, s)[...] * g1 for s in range(ROW_TILES)], axis=1)
    out = _layer_norm(z, g_ref[...], b_ref[...])
    o_ref[...] = out
    ob_ref[...] = out.astype(ob_ref.dtype)


def _combine(dest, y_pad, route, h, g, b, tm=128):
    T = route.shape[0]
    D = g.shape[1]
    n = T // tm
    row = lambda i: (i, 0)
    const = lambda i: (0, 0)
    dest3 = dest.reshape(n, 1, 2 * tm)
    return pl.pallas_call(
        _combine_kernel,
        grid=(n,),
        in_specs=[pl.BlockSpec((1, 1, 2 * tm), lambda i: (i, 0, 0), memory_space=pltpu.SMEM),
                  pl.BlockSpec((1, 1, 2 * tm), lambda i: (jnp.minimum(i + 1, n - 1), 0, 0),
                               memory_space=pltpu.SMEM),
                  pl.BlockSpec(memory_space=pl.ANY),
                  pl.BlockSpec((tm, LANES), row),
                  pl.BlockSpec((tm * ROW_TILES, LANES), row),
                  pl.BlockSpec((1, D), const),
                  pl.BlockSpec((1, D), const)],
        out_specs=[pl.BlockSpec((tm, D), row), pl.BlockSpec((tm, D), row)],
        out_shape=[jax.ShapeDtypeStruct((T, D), jnp.float32),
                   jax.ShapeDtypeStruct((T, D), _MXU_DTYPE)],
        scratch_shapes=[pltpu.VMEM((2, 2, tm * ROW_TILES, LANES), jnp.float32),
                        pltpu.SemaphoreType.DMA((2, 2))],
        compiler_params=_cparams(("arbitrary",)),
        name="combine_ln",
    )(dest3, dest3, y_pad, route, h, g, b)


def _spread64(wt):
    n, K = wt.shape[0] // IDX_D, wt.shape[1]
    wt = wt.reshape(n, 2, IDX_D // 2, K)
    wt = jnp.pad(wt, ((0, 0), (0, 0), (0, IDX_D // 2), (0, 0)))
    return wt.reshape(n * LANES, K)


_C_RQ = 0
_C_RV = 2 * RET_HEADS * RET_DK
_C_AQ = _C_RV + 2 * RET_HEADS * RET_DV
_C_AK = _C_AQ + ATT_HEADS * ATT_HD
_C_IQ = _C_AK + 2 * ATT_HD
_C_IK = _C_IQ + IDX_HEADS * IDX_D
_C_IW = _C_IK + IDX_D
_C_GR = _C_IW + IDX_HEADS
_C_END = _C_GR + 2 * D_MODEL_


def _moe_plan(route, counts):
    T = route.shape[0]
    n_assign = T * 2
    n_blocks = -(-n_assign // FFN_BLK) + N_EXP
    cnt = counts[0, N_GRP:N_GRP + N_EXP].astype(jnp.int32)
    padded = (cnt + FFN_BLK - 1) // FFN_BLK * FFN_BLK
    pad_ends = jnp.cumsum(padded)
    pad_starts = pad_ends - padded
    expert = route[:, 0:2].astype(jnp.int32)
    rank = route[:, 2:4].astype(jnp.int32)
    is_e = expert[:, :, None] == jnp.arange(N_EXP, dtype=jnp.int32)
    dest = (jnp.sum(jnp.where(is_e, pad_starts, 0), axis=-1) + rank).reshape(n_assign)
    tok = jnp.repeat(jnp.arange(T, dtype=jnp.int32), 2)
    tok_pad = jnp.zeros((n_blocks * FFN_BLK,), jnp.int32).at[dest].set(tok)
    block_start = jnp.arange(n_blocks, dtype=jnp.int32) * FFN_BLK
    block_expert = jnp.minimum(
        jnp.sum((pad_ends[None, :] <= block_start[:, None]).astype(jnp.int32), axis=1), N_EXP - 1)
    return dest, tok_pad, block_expert


def kernel(x, w_in, w_ret_o, w_dsa_o, w_out, ln1_g, ln1_b, router_grp_w, router_grp_b,
           router_exp_w, router_exp_b, w_gate, w_up, w_down, ln2_g, ln2_b):
    B, S, D = x.shape
    T = B * S
    cdt = _MXU_DTYPE
    assert w_in.shape[-1] == _C_END
    t_ret = [jnp.asarray(t) for t in _rot_tables(S, RET_DK, 4 * RET_DK, "half")]
    t_att = [jnp.asarray(t) for t in _rot_tables(S, ATT_HD, 8 * ATT_HD, "half")]
    t_idx = [jnp.asarray(t) for t in _rot_tables(S, IDX_D, 8 * LANES, "spread")]
    ident = (np.ones((S, LANES), np.float32), np.zeros((S, LANES), np.float32))
    t_e = [jnp.asarray(np.concatenate([ta[:, :LANES], i, ti[:, :LANES], i], axis=1))
           for ta, ti, i in zip(_rot_tables(S, ATT_HD, LANES, "half"),
                                _rot_tables(S, IDX_D, LANES, "spread"), ident)]
    xf = x.reshape(T, D)
    xb = xf.astype(cdt)
    w_in_t = jnp.swapaxes(w_in, 1, 2)
    for l in range(DEPTH_):
        w_iq = _spread64(w_in_t[l, _C_IQ:_C_IK])
        w_e = jnp.concatenate([w_in_t[l, _C_AK:_C_IQ], _spread64(w_in_t[l, _C_IK:_C_IW]),
                               jnp.pad(w_in_t[l, _C_IW:_C_GR], ((0, LANES - IDX_HEADS), (0, 0)))],
                              axis=0)
        w_gates = w_in_t[l, _C_GR:_C_END]
        qk = _proj(xb, w_in_t, B, S, layer=l, col0=_C_RQ, n_cols=_C_RV - _C_RQ, rot=RET_DK,
                   tables=t_ret)
        vg = _proj(xb, w_in_t, B, S, layer=l, col0=_C_RV, n_cols=_C_AQ - _C_RV)
        aq = _proj(xb, w_in_t, B, S, layer=l, col0=_C_AQ, n_cols=_C_AK - _C_AQ, rot=ATT_HD,
                   tables=t_att)
        iq = _proj(xb, w_iq, B, S, rot=LANES, tables=t_idx)
        e = _proj(xb, w_e, B, S, rot=LANES, tables=t_e, out_dtype=jnp.float32)
        gates = _proj(xb, w_gates, B, S)
        ret = _retention(qk, vg, B, S)
        att = _dsa(aq, iq, e, B, S)
        rw = jnp.pad(jnp.concatenate([router_grp_w[l], router_exp_w[l]], axis=1),
                     ((0, 0), (0, LANES - N_GRP - N_EXP)))
        rb = jnp.pad(jnp.concatenate([router_grp_b[l], router_exp_b[l]]),
                     (0, LANES - N_GRP - N_EXP)).reshape(1, LANES)
        h, route, counts = _merge(ret, att, gates, xf, w_ret_o[l].astype(cdt),
                                  w_dsa_o[l].astype(cdt), w_out[l].astype(cdt),
                                  ln1_g[l].reshape(1, D), ln1_b[l].reshape(1, D), rw, rb)
        dest, tok_pad, block_expert = _moe_plan(route, counts)
        y_pad = _ffn(block_expert, tok_pad, h, w_gate, w_up, w_down, l)
        xf, xb = _combine(dest, y_pad, route, h, ln2_g[l].reshape(1, D), ln2_b[l].reshape(1, D))
    return xf.reshape(B, S, D)
```

```python
import functools

import numpy as np
import jax
import jax.numpy as jnp
from jax import lax
from jax.experimental import pallas as pl
from jax.experimental.pallas import tpu as pltpu

D_MODEL_ = 1024
DEPTH_ = 2
RET_HEADS = 4
RET_DK = 256
RET_DV = 512
RET_C = 128
ATT_HEADS = 8
ATT_HD = 128
IDX_HEADS = 8
IDX_D = 64
TOPK = 256
QBLK = 128
N_GRP = 4
EXP_PER_GRP = 8
N_EXP = N_GRP * EXP_PER_GRP
EXP_FF = 512
FFN_BLK = 128
THETA = 10000.0
LN_EPS_ = 1e-5
GN_EPS_ = 1e-5
ALPHA = (2 * DEPTH_) ** 0.25

LANES = 128
SUBLANES = 8
INT_MIN = -(2 ** 31)
_CODE_NEG_INF = INT_MIN + 0x7FFFFF
_F32_LOWEST = float(np.finfo(np.float32).min)

_MXU_DTYPE = jnp.bfloat16
_VMEM_LIMIT = 56 * 1024 * 1024

ROW_TILES = D_MODEL_ // LANES
assert ROW_TILES == SUBLANES


def _cparams(sem):
    return pltpu.CompilerParams(dimension_semantics=sem, vmem_limit_bytes=_VMEM_LIMIT)


def _dot(a, b):
    return jnp.dot(a, b, preferred_element_type=jnp.float32)


def _dot_nt(a, b):
    return lax.dot_general(a, b, (((1,), (1,)), ((), ())), preferred_element_type=jnp.float32)


def _dot_tn(a, b):
    return lax.dot_general(a, b, (((0,), (0,)), ((), ())), preferred_element_type=jnp.float32)


def _tile_rows(ref, n, s):
    return ref.at[pl.ds(s, n, stride=ROW_TILES), :]


def _load_tiled(ref, n):
    return jnp.concatenate([_tile_rows(ref, n, s)[...] for s in range(ROW_TILES)], axis=1)


def _store_tiled(ref, val):
    n = val.shape[0]
    for s in range(ROW_TILES):
        _tile_rows(ref, n, s)[...] = val[:, s * LANES:(s + 1) * LANES]


def _swap_halves(a, d):
    tn = a.shape[1]
    if d == 2 * LANES:
        parts = []
        for c in range(0, tn, d):
            parts += [a[:, c + LANES:c + d], a[:, c:c + LANES]]
        return jnp.concatenate(parts, axis=1)
    assert d == LANES
    return jnp.concatenate(
        [pltpu.roll(a[:, c:c + LANES], LANES // 2, axis=1) for c in range(0, tn, LANES)], axis=1)


def _proj_kernel(x_ref, w_ref, *rest, rot):
    if rot:
        c_ref, s_ref, o_ref, w_s = rest
    else:
        o_ref, w_s = rest

    @pl.when((pl.program_id(1) == 0) & (pl.program_id(2) == 0))
    def _():
        w_s[...] = w_ref[...].astype(w_s.dtype)

    acc = _dot_nt(x_ref[...], w_s[...])
    if rot:
        acc = acc * c_ref[...] + _swap_halves(acc, rot) * s_ref[...]
    o_ref[...] = acc.astype(o_ref.dtype)


def _proj(x, wt, batch, seq, *, layer=None, col0=0, n_cols=None, rot=0, tables=None,
          out_dtype=None, tm=1024, tn=1024):
    T, K = x.shape
    n_cols = n_cols or wt.shape[-2]
    tn = min(tn, n_cols)
    tm = min(tm, seq)
    n_seq = seq // tm
    assert col0 % tn == 0 and n_cols % tn == 0 and seq % tm == 0
    c0 = col0 // tn
    row = lambda j, p, b: (b * n_seq + p, 0)
    if layer is None:
        w_spec = pl.BlockSpec((tn, K), lambda j, p, b: (c0 + j, 0))
    else:
        w_spec = pl.BlockSpec((None, tn, K), lambda j, p, b: (layer, c0 + j, 0))
    in_specs = [pl.BlockSpec((tm, K), row), w_spec]
    args = [x, wt]
    if rot:
        in_specs += [pl.BlockSpec((tm, tn), lambda j, p, b: (p, 0))] * 2
        args += list(tables)
    return pl.pallas_call(
        functools.partial(_proj_kernel, rot=rot),
        grid=(n_cols // tn, n_seq, batch),
        in_specs=in_specs,
        out_specs=pl.BlockSpec((tm, tn), lambda j, p, b: (b * n_seq + p, j)),
        out_shape=jax.ShapeDtypeStruct((T, n_cols), out_dtype or x.dtype),
        scratch_shapes=[pltpu.VMEM((tn, K), _MXU_DTYPE)],
        compiler_params=_cparams(("parallel", "arbitrary", "arbitrary")),
        name="proj_rot%d" % rot,
    )(*args)


def _rot_tables(seq, d, width, layout):
    inv = 1.0 / (THETA ** (np.arange(0, d, 2, dtype=np.float64) / d))
    ang = np.arange(seq, dtype=np.float64)[:, None] * inv[None, :]
    cos, sin = np.cos(ang), np.sin(ang)
    if layout == "half":
        c = np.concatenate([cos, cos], axis=1)
        s = np.concatenate([-sin, sin], axis=1)
    else:
        z = np.zeros_like(cos)
        c = np.concatenate([cos, z, cos, z], axis=1)
        s = np.concatenate([-sin, z, sin, z], axis=1)
    reps = width // c.shape[1]
    return np.tile(c, (1, reps)).astype(np.float32), np.tile(s, (1, reps)).astype(np.float32)


_RET_CHUNKS_PER_STEP = 4


def _ret_kernel(qk_ref, v_ref, g_ref, dec_ref, xi_ref, zeta_ref, o_ref, r_ref, *, gamma_c):
    H, DK, DV, C = RET_HEADS, RET_DK, RET_DV, RET_C

    @pl.when(pl.program_id(1) == 0)
    def _():
        r_ref[...] = jnp.zeros_like(r_ref)

    for h in range(H):
        xi = jnp.concatenate([xi_ref[h]] * (DV // LANES), axis=1)
        zeta = jnp.concatenate([zeta_ref[h]] * (DV // LANES), axis=1)
        for c in range(qk_ref.shape[0] // C):
            rows = slice(c * C, (c + 1) * C)
            q = qk_ref[rows, h * DK:(h + 1) * DK]
            k = qk_ref[rows, (H + h) * DK:(H + h + 1) * DK]
            v = v_ref[rows, h * DV:(h + 1) * DV]
            s = _dot_nt(q, k) * dec_ref[h]
            intra = _dot(s.astype(_MXU_DTYPE), v)
            r_old = r_ref[h]
            cross = _dot(q, r_old.astype(_MXU_DTYPE))
            y = intra + cross * xi
            vz = (v.astype(jnp.float32) * zeta).astype(_MXU_DTYPE)
            r_ref[h] = gamma_c[h] * r_old + _dot_tn(k, vz)
            mu = jnp.mean(y, axis=-1, keepdims=True)
            yc = y - mu
            var = jnp.mean(yc * yc, axis=-1, keepdims=True)
            yn = yc * lax.rsqrt(var + GN_EPS_)
            g = g_ref[rows, h * DV:(h + 1) * DV].astype(jnp.float32)
            o_ref[rows, h * DV:(h + 1) * DV] = (yn * (g * jax.nn.sigmoid(g))).astype(o_ref.dtype)


def _retention(qk, vg, batch, seq):
    H, DK, DV, C = RET_HEADS, RET_DK, RET_DV, RET_C
    rows = C * _RET_CHUNKS_PER_STEP
    n_steps = seq // rows
    lg = np.log(1.0 - 2.0 ** (-5.0 - np.arange(H, dtype=np.float64)))
    pos = np.arange(C, dtype=np.float64)
    diff = pos[:, None] - pos[None, :]
    dec = np.where(diff >= 0, np.exp(np.maximum(diff, 0.0)[None] * lg[:, None, None]), 0.0)
    dec = (dec * DK ** -0.5).astype(np.float32)
    ones = np.ones((1, 1, LANES))
    xi = (np.exp((pos[None, :] + 1.0) * lg[:, None])[:, :, None] * ones).astype(np.float32)
    zeta = (np.exp((C - 1.0 - pos)[None, :] * lg[:, None])[:, :, None] * ones
            * DK ** -0.5).astype(np.float32)
    gamma_c = tuple(float(g) for g in np.exp(C * lg))
    T = qk.shape[0]
    const3 = lambda b, n: (0, 0, 0)
    row = lambda b, n: (b * n_steps + n, 0)
    return pl.pallas_call(
        functools.partial(_ret_kernel, gamma_c=gamma_c),
        grid=(batch, n_steps),
        in_specs=[pl.BlockSpec((rows, 2 * H * DK), row),
                  pl.BlockSpec((rows, H * DV), row),
                  pl.BlockSpec((rows, H * DV), lambda b, n: (b * n_steps + n, 1)),
                  pl.BlockSpec((H, C, C), const3),
                  pl.BlockSpec((H, C, LANES), const3),
                  pl.BlockSpec((H, C, LANES), const3)],
        out_specs=pl.BlockSpec((rows, H * DV), row),
        out_shape=jax.ShapeDtypeStruct((T, H * DV), qk.dtype),
        scratch_shapes=[pltpu.VMEM((H, DK, DV), jnp.float32)],
        compiler_params=_cparams(("parallel", "arbitrary")),
        name="retention",
    )(qk, vg, vg, jnp.asarray(dec), jnp.asarray(xi), jnp.asarray(zeta))


_COUNT_CHAINS = 8
_DSA_CLASSES = 8


def _col_count(mask):
    n, qb = mask.shape
    part = jnp.sum(mask.astype(jnp.int32).reshape(_COUNT_CHAINS, n // _COUNT_CHAINS, qb), axis=1)
    return jnp.sum(part, axis=0, keepdims=True)


def _dsa_block(sv, j, aq_ref, iq_ref, eq_ref, o_ref, k_s, ik_s, vt_s, score_s, bias_s, topk):
    QB = aq_ref.shape[0]
    w_t = eq_ref[:, 3 * LANES:4 * LANES].T * ((IDX_D ** -0.5) * (IDX_HEADS ** -0.5))
    score = jnp.zeros((sv, QB), jnp.float32)
    for h in range(IDX_HEADS):
        rel = jnp.maximum(_dot_nt(ik_s[0:sv, :], iq_ref[:, h * LANES:(h + 1) * LANES]), 0.0)
        score = score + rel * w_t[h:h + 1, :]
    key_pos = lax.broadcasted_iota(jnp.int32, (sv, QB), 0)
    q_pos = j * QB + lax.broadcasted_iota(jnp.int32, (sv, QB), 1)
    score_s[0:sv, :] = jnp.where(key_pos <= q_pos, score, -jnp.inf)

    def count_ge(c):
        return _col_count(score_s[0:sv, :] >= c)

    def float_of(code):
        return pltpu.bitcast(jnp.where(code < 0, code ^ 0x7FFFFFFF, code), jnp.float32)

    code = jnp.where(count_ge(jnp.zeros((1, QB), jnp.float32)) >= topk, 0, INT_MIN)

    def bit_step(i, code):
        cand = code | lax.shift_left(jnp.int32(1), 30 - i)
        ok = (count_ge(float_of(cand)) >= topk) | (cand <= _CODE_NEG_INF)
        return jnp.where(ok, cand, code)

    code = lax.fori_loop(0, 31, bit_step, code)
    thr = float_of(code)
    thr_lo = jnp.maximum(thr, _F32_LOWEST)
    has_tie = jnp.max(jnp.where(code > _CODE_NEG_INF, count_ge(thr) - topk, 0)) > 0

    @pl.when(jnp.logical_not(has_tie))
    def _():
        bias_s[0:sv, :] = jnp.where(score_s[0:sv, :] >= thr_lo, 0.0, -jnp.inf)

    @pl.when(has_tie)
    def _():
        need_f = (topk - _col_count(score_s[0:sv, :] > thr)).astype(jnp.float32)
        r_i = lax.broadcasted_iota(jnp.int32, (LANES, LANES), 0)
        c_i = lax.broadcasted_iota(jnp.int32, (LANES, LANES), 1)
        lower = (r_i > c_i).astype(_MXU_DTYPE)
        carry = jnp.zeros((1, QB), jnp.float32)
        for c in range(sv // LANES):
            sl = slice(c * LANES, (c + 1) * LANES)
            sc = score_s[sl, :]
            eq_c = sc == thr
            eq_f = eq_c.astype(jnp.float32)
            before = _dot(lower, eq_f.astype(_MXU_DTYPE)) + carry
            take = ((sc > thr) | (eq_c & (before < need_f))) & (sc >= _F32_LOWEST)
            bias_s[sl, :] = jnp.where(take, 0.0, -jnp.inf)
            carry = carry + jnp.sum(eq_f, axis=0, keepdims=True)

    c_exp = (ATT_HD ** -0.5) * float(np.log2(np.e))
    for h in range(ATT_HEADS):
        s = _dot_nt(k_s[0:sv, :], aq_ref[:, h * ATT_HD:(h + 1) * ATT_HD]) + bias_s[0:sv, :]
        m = jnp.max(s, axis=0, keepdims=True)
        p = jnp.exp2((s - m) * c_exp)
        l = jnp.sum(p, axis=0, keepdims=True)
        o_t = _dot(vt_s[:, 0:sv], p.astype(_MXU_DTYPE)) / l
        o_ref[:, h * ATT_HD:(h + 1) * ATT_HD] = o_t.T.astype(o_ref.dtype)


def _dsa_kernel(aq_ref, iq_ref, eq_ref, ek_ref, o_ref, k_s, ik_s, vt_s, score_s, bias_s, *, topk):
    S = ek_ref.shape[0]
    QB = aq_ref.shape[0]
    j = pl.program_id(1)

    @pl.when(j == 0)
    def _():
        k_s[...] = ek_ref[:, 0:LANES].astype(k_s.dtype)
        vt_s[...] = ek_ref[:, LANES:2 * LANES].T.astype(vt_s.dtype)
        ik_s[...] = ek_ref[:, 2 * LANES:3 * LANES].astype(ik_s.dtype)

    n_blk = S // QB
    n_cls = min(_DSA_CLASSES, n_blk)
    per_class = n_blk // n_cls
    for c in range(n_cls):
        @pl.when((j >= c * per_class) & (j < (c + 1) * per_class))
        def _(c=c):
            _dsa_block((c + 1) * per_class * QB, j, aq_ref, iq_ref, eq_ref, o_ref,
                       k_s, ik_s, vt_s, score_s, bias_s, topk)


def _dsa(aq, iq, e, batch, seq):
    T = aq.shape[0]
    nb = seq // QBLK
    topk = min(TOPK, seq // 4)
    qrow = lambda b, j: (b * nb + j, 0)
    return pl.pallas_call(
        functools.partial(_dsa_kernel, topk=topk),
        grid=(batch, nb),
        in_specs=[pl.BlockSpec((QBLK, ATT_HEADS * ATT_HD), qrow),
                  pl.BlockSpec((QBLK, IDX_HEADS * LANES), qrow),
                  pl.BlockSpec((QBLK, 4 * LANES), qrow),
                  pl.BlockSpec((seq, 4 * LANES), lambda b, j: (b, 0))],
        out_specs=pl.BlockSpec((QBLK, ATT_HEADS * ATT_HD), qrow),
        out_shape=jax.ShapeDtypeStruct((T, ATT_HEADS * ATT_HD), aq.dtype),
        scratch_shapes=[pltpu.VMEM((seq, LANES), aq.dtype),
                        pltpu.VMEM((seq, LANES), aq.dtype),
                        pltpu.VMEM((LANES, seq), aq.dtype),
                        pltpu.VMEM((seq, QBLK), jnp.float32),
                        pltpu.VMEM((seq, QBLK), jnp.float32)],
        compiler_params=_cparams(("parallel", "arbitrary")),
        name="dsa",
    )(aq, iq, e, e)


def _layer_norm(z, g, b):
    mu = jnp.mean(z, axis=-1, keepdims=True)
    zc = z - mu
    var = jnp.mean(zc * zc, axis=-1, keepdims=True)
    return zc * lax.rsqrt(var + LN_EPS_) * g + b


def _split_hi_lo(a):
    hi = a.astype(_MXU_DTYPE)
    lo = (a - hi.astype(jnp.float32)).astype(_MXU_DTYPE)
    return hi, lo


def _merge_kernel(ret_ref, att_ref, gr_ref, ga_ref, x_ref, wr_ref, wd_ref, wo_ref, g_ref, b_ref,
                  rwh_ref, rwl_ref, rb_ref, h_ref, route_ref, cnt_ref, carry_s):
    @pl.when(pl.program_id(0) == 0)
    def _():
        carry_s[...] = jnp.zeros_like(carry_s)

    a = _dot(ret_ref[...], wr_ref[...])
    d = _dot(att_ref[...], wd_ref[...])
    merged = (jax.nn.sigmoid(gr_ref[...].astype(jnp.float32)) * a
              + jax.nn.sigmoid(ga_ref[...].astype(jnp.float32)) * d)
    mix = _dot(merged.astype(_MXU_DTYPE), wo_ref[...])
    h = _layer_norm(ALPHA * x_ref[...] + mix, g_ref[...], b_ref[...])
    _store_tiled(h_ref, h)
    h_hi, h_lo = _split_hi_lo(h)
    logits = (_dot(h_hi, rwh_ref[...]) + _dot(h_lo, rwh_ref[...]) + _dot(h_hi, rwl_ref[...])
              + rb_ref[...])
    route_ref[...] = _route_tile(logits, carry_s)
    cnt_ref[...] = carry_s[...]


def _merge(ret, att, gates, x, wr, wd, wo, g, b, rw, rb, tm=512):
    T, D = x.shape
    row = lambda i: (i, 0)
    const = lambda i: (0, 0)
    rw_hi, rw_lo = _split_hi_lo(rw)
    return pl.pallas_call(
        _merge_kernel,
        grid=(T // tm,),
        in_specs=[pl.BlockSpec((tm, ret.shape[1]), row),
                  pl.BlockSpec((tm, att.shape[1]), row),
                  pl.BlockSpec((tm, D), lambda i: (i, 0)),
                  pl.BlockSpec((tm, D), lambda i: (i, 1)),
                  pl.BlockSpec((tm, D), row),
                  pl.BlockSpec(wr.shape, const),
                  pl.BlockSpec(wd.shape, const),
                  pl.BlockSpec(wo.shape, const),
                  pl.BlockSpec((1, D), const),
                  pl.BlockSpec((1, D), const),
                  pl.BlockSpec(rw.shape, const),
                  pl.BlockSpec(rw.shape, const),
                  pl.BlockSpec((1, LANES), const)],
        out_specs=[pl.BlockSpec((tm * ROW_TILES, LANES), row), pl.BlockSpec((tm, LANES), row),
                   pl.BlockSpec((SUBLANES, LANES), const)],
        out_shape=[jax.ShapeDtypeStruct((T * ROW_TILES, LANES), jnp.float32),
                   jax.ShapeDtypeStruct((T, LANES), jnp.float32),
                   jax.ShapeDtypeStruct((SUBLANES, LANES), jnp.float32)],
        scratch_shapes=[pltpu.VMEM((SUBLANES, LANES), jnp.float32)],
        compiler_params=_cparams(("arbitrary",)),
        name="merge_ln_router",
    )(ret, att, gates, gates, x, wr, wd, wo, g, b, rw_hi, rw_lo, rb)


def _route_tile(lg, carry_s):
    tm = lg.shape[0]
    lane = lax.broadcasted_iota(jnp.int32, (tm, LANES), 1).astype(jnp.float32)
    neg = -jnp.inf

    def first_max(v):
        m = jnp.max(v, axis=-1, keepdims=True)
        idx = jnp.min(jnp.where(v == m, lane, float(LANES)), axis=-1, keepdims=True)
        return m, idx

    gl = jnp.where(lane < N_GRP, lg, neg)
    gmax, gsel = first_max(gl)
    p_grp = 1.0 / jnp.sum(jnp.exp(gl - gmax), axis=-1, keepdims=True)
    lo = N_GRP + gsel * EXP_PER_GRP
    el = jnp.where((lane >= lo) & (lane < lo + EXP_PER_GRP), lg, neg)
    top1, i1 = first_max(el)
    top2, i2 = first_max(jnp.where(lane == i1, neg, el))
    z = jnp.exp(top2 - top1)
    g1 = p_grp / (1.0 + z)
    g2 = p_grp * z / (1.0 + z)
    oh1 = lane == i1
    oh2 = lane == i2
    oh = (oh1 | oh2).astype(jnp.float32)
    r_i = lax.broadcasted_iota(jnp.int32, (tm, tm), 0)
    c_i = lax.broadcasted_iota(jnp.int32, (tm, tm), 1)
    lower = (r_i > c_i).astype(_MXU_DTYPE)
    before = _dot(lower, oh.astype(_MXU_DTYPE)) + carry_s[0:1, :]
    rank1 = jnp.sum(jnp.where(oh1, before, 0.0), axis=-1, keepdims=True)
    rank2 = jnp.sum(jnp.where(oh2, before, 0.0), axis=-1, keepdims=True)
    new_carry = carry_s[0:1, :] + jnp.sum(oh, axis=0, keepdims=True)
    carry_s[...] = jnp.broadcast_to(new_carry, carry_s.shape)
    e1 = i1 - N_GRP
    e2 = i2 - N_GRP
    out = jnp.zeros((tm, LANES), jnp.float32)
    for col, val in enumerate((e1, e2, rank1, rank2, g1, g2)):
        out = jnp.where(lane == col, val, out)
    return out


_GATHER_UNROLL = 8


def _token_copy(src_hbm, dst_vmem, sem, src_tok, dst_tok):
    src = src_hbm.at[pl.ds(pl.multiple_of(src_tok * ROW_TILES, ROW_TILES), ROW_TILES)]
    dst = dst_vmem.at[pl.ds(pl.multiple_of(dst_tok * ROW_TILES, ROW_TILES), ROW_TILES)]
    return pltpu.make_async_copy(src, dst, sem)


def _start_token_gather(idx_ref, src_hbm, dst_vmem, sem, n, stride=1, offset=0):
    def issue(r, c):
        _token_copy(src_hbm, dst_vmem, sem, idx_ref[0, 0, offset + r * stride], r).start()
        return c

    lax.fori_loop(0, n, issue, 0, unroll=_GATHER_UNROLL)


def _wait_token_gather(src_hbm, dst_vmem, sem, n):
    pltpu.make_async_copy(src_hbm.at[pl.ds(0, n * ROW_TILES)], dst_vmem, sem).wait()


def _dispatch_kernel(last_ref, dest_ref, h_ref, x_hbm, zero_s, sem):
    tm = h_ref.shape[0] // ROW_TILES
    blk_rows = zero_s.shape[0]
    n_blocks = x_hbm.shape[0] // blk_rows

    @pl.when(pl.program_id(0) == 0)
    def _():
        zero_s[...] = jnp.zeros_like(zero_s)

        def zero_block(blk):
            cp = pltpu.make_async_copy(
                zero_s, x_hbm.at[pl.ds(pl.multiple_of(blk * blk_rows, ROW_TILES), blk_rows)], sem)
            cp.start()
            cp.wait()

        def fill(e, c):
            blk = last_ref[e]

            @pl.when(blk >= 0)
            def _():
                zero_block(blk)
            return c

        lax.fori_loop(0, N_EXP, fill, 0)

        def fill_tail(blk, c):
            zero_block(blk)
            return c

        lax.fori_loop(last_ref[N_EXP], n_blocks, fill_tail, 0)

    def issue(r, c):
        src = h_ref.at[pl.ds(pl.multiple_of(r * ROW_TILES, ROW_TILES), ROW_TILES)]
        for j in range(2):
            d = dest_ref[0, 0, 2 * r + j]
            dst = x_hbm.at[pl.ds(pl.multiple_of(d * ROW_TILES, ROW_TILES), ROW_TILES)]
            pltpu.make_async_copy(src, dst, sem).start()
        return c

    lax.fori_loop(0, tm, issue, 0, unroll=_GATHER_UNROLL // 2)
    for _ in range(2):
        pltpu.make_async_copy(h_ref, x_hbm.at[pl.ds(0, tm * ROW_TILES)], sem).wait()


def _dispatch(dest, last_blk, h, n_blocks, tm=128):
    T = h.shape[0] // ROW_TILES
    n = T // tm
    blk_rows = FFN_BLK * ROW_TILES
    grid_spec = pltpu.PrefetchScalarGridSpec(
        num_scalar_prefetch=1,
        grid=(n,),
        in_specs=[pl.BlockSpec((1, 1, 2 * tm), lambda i, lb: (i, 0, 0), memory_space=pltpu.SMEM),
                  pl.BlockSpec((tm * ROW_TILES, LANES), lambda i, lb: (i, 0))],
        out_specs=pl.BlockSpec(memory_space=pl.ANY),
        scratch_shapes=[pltpu.VMEM((blk_rows, LANES), jnp.float32), pltpu.SemaphoreType.DMA(())],
    )
    return pl.pallas_call(
        _dispatch_kernel,
        grid_spec=grid_spec,
        out_shape=jax.ShapeDtypeStruct((n_blocks * blk_rows, LANES), jnp.float32),
        compiler_params=_cparams(("arbitrary",)),
        name="dispatch",
    )(last_blk, dest.reshape(n, 1, 2 * tm), h)


def _ffn_kernel(be_ref, x_ref, wg_ref, wu_ref, wd_ref, y_ref, wg_s, wu_s, wd_s):
    i = pl.program_id(0)

    @pl.when((i == 0) | (be_ref[i] != be_ref[jnp.maximum(i - 1, 0)]))
    def _():
        wg_s[...] = wg_ref[0].astype(wg_s.dtype)
        wu_s[...] = wu_ref[0].astype(wu_s.dtype)
        wd_s[...] = wd_ref[0].astype(wd_s.dtype)

    xb = _load_tiled(x_ref, FFN_BLK).astype(_MXU_DTYPE)
    gate = _dot(xb, wg_s[...])
    up = _dot(xb, wu_s[...])
    act = (gate * jax.nn.sigmoid(gate)) * up
    _store_tiled(y_ref, _dot(act.astype(_MXU_DTYPE), wd_s[...]))


def _ffn(block_expert, x_pad, wg, wu, wd, layer):
    n_blocks = block_expert.shape[0]
    D, F = wg.shape[2], wg.shape[3]
    wmap = lambda i, be: (layer, be[i], 0, 0)
    blk_rows = FFN_BLK * ROW_TILES
    blk = pl.BlockSpec((blk_rows, LANES), lambda i, be: (i, 0))
    grid_spec = pltpu.PrefetchScalarGridSpec(
        num_scalar_prefetch=1,
        grid=(n_blocks,),
        in_specs=[blk,
                  pl.BlockSpec((None, 1, D, F), wmap),
                  pl.BlockSpec((None, 1, D, F), wmap),
                  pl.BlockSpec((None, 1, F, D), wmap)],
        out_specs=blk,
        scratch_shapes=[pltpu.VMEM((D, F), _MXU_DTYPE),
                        pltpu.VMEM((D, F), _MXU_DTYPE),
                        pltpu.VMEM((F, D), _MXU_DTYPE)],
    )
    return pl.pallas_call(
        _ffn_kernel,
        grid_spec=grid_spec,
        out_shape=jax.ShapeDtypeStruct((n_blocks * blk_rows, LANES), jnp.float32),
        compiler_params=_cparams(("arbitrary",)),
        name="expert_ffn",
    )(block_expert, x_pad, wg, wu, wd)


def _combine_kernel(dest_ref, dest_next_ref, y_hbm, r_ref, h_ref, g_ref, b_ref, o_ref, ob_ref,
                    y_buf, sem):
    tm = o_ref.shape[0]
    i = pl.program_id(0)
    n = pl.num_programs(0)
    slot = i % 2

    def start(idx_ref, s):
        for c in range(2):
            _start_token_gather(idx_ref, y_hbm, y_buf.at[s, c], sem.at[s, c], tm, stride=2, offset=c)

    @pl.when(i == 0)
    def _():
        start(dest_ref, 0)

    @pl.when(i + 1 < n)
    def _():
        start(dest_next_ref, 1 - slot)

    for c in range(2):
        _wait_token_gather(y_hbm, y_buf.at[slot, c], sem.at[slot, c], tm)
    r = r_ref[...]
    g0 = r[:, 4:5]
    g1 = r[:, 5:6]
    z = jnp.concatenate(
        [ALPHA * _tile_rows(h_ref, tm, s)[...]
         + _tile_rows(y_buf.at[slot, 0], tm, s)[...] * g0
         + _tile_rows(y_buf.at[slot, 1], tm, s)[...] * g1 for s in range(ROW_TILES)], axis=1)
    out = _layer_norm(z, g_ref[...], b_ref[...])
    o_ref[...] = out
    ob_ref[...] = out.astype(ob_ref.dtype)


def _combine(dest, y_pad, route, h, g, b, tm=128):
    T = route.shape[0]
    D = g.shape[1]
    n = T // tm
    row = lambda i: (i, 0)
    const = lambda i: (0, 0)
    dest3 = dest.reshape(n, 1, 2 * tm)
    return pl.pallas_call(
        _combine_kernel,
        grid=(n,),
        in_specs=[pl.BlockSpec((1, 1, 2 * tm), lambda i: (i, 0, 0), memory_space=pltpu.SMEM),
                  pl.BlockSpec((1, 1, 2 * tm), lambda i: (jnp.minimum(i + 1, n - 1), 0, 0),
                               memory_space=pltpu.SMEM),
                  pl.BlockSpec(memory_space=pl.ANY),
                  pl.BlockSpec((tm, LANES), row),
                  pl.BlockSpec((tm * ROW_TILES, LANES), row),
                  pl.BlockSpec((1, D), const),
                  pl.BlockSpec((1, D), const)],
        out_specs=[pl.BlockSpec((tm, D), row), pl.BlockSpec((tm, D), row)],
        out_shape=[jax.ShapeDtypeStruct((T, D), jnp.float32),
                   jax.ShapeDtypeStruct((T, D), _MXU_DTYPE)],
        scratch_shapes=[pltpu.VMEM((2, 2, tm * ROW_TILES, LANES), jnp.float32),
                        pltpu.SemaphoreType.DMA((2, 2))],
        compiler_params=_cparams(("arbitrary",)),
        name="combine_ln",
    )(dest3, dest3, y_pad, route, h, g, b)


def _spread64(wt):
    n, K = wt.shape[0] // IDX_D, wt.shape[1]
    wt = wt.reshape(n, 2, IDX_D // 2, K)
    wt = jnp.pad(wt, ((0, 0), (0, 0), (0, IDX_D // 2), (0, 0)))
    return wt.reshape(n * LANES, K)


_C_RQ = 0
_C_RV = 2 * RET_HEADS * RET_DK
_C_AQ = _C_RV + 2 * RET_HEADS * RET_DV
_C_AK = _C_AQ + ATT_HEADS * ATT_HD
_C_IQ = _C_AK + 2 * ATT_HD
_C_IK = _C_IQ + IDX_HEADS * IDX_D
_C_IW = _C_IK + IDX_D
_C_GR = _C_IW + IDX_HEADS
_C_END = _C_GR + 2 * D_MODEL_


def _moe_plan(route, counts):
    T = route.shape[0]
    n_assign = T * 2
    n_blocks = -(-n_assign // FFN_BLK) + N_EXP
    cnt = counts[0, N_GRP:N_GRP + N_EXP].astype(jnp.int32)
    padded = (cnt + FFN_BLK - 1) // FFN_BLK * FFN_BLK
    pad_ends = jnp.cumsum(padded)
    pad_starts = pad_ends - padded
    expert = route[:, 0:2].astype(jnp.int32)
    rank = route[:, 2:4].astype(jnp.int32)
    is_e = expert[:, :, None] == jnp.arange(N_EXP, dtype=jnp.int32)
    dest = (jnp.sum(jnp.where(is_e, pad_starts, 0), axis=-1) + rank).reshape(n_assign)
    last_blk = jnp.concatenate([jnp.where(padded > 0, pad_ends // FFN_BLK - 1, -1),
                                pad_ends[-1:] // FFN_BLK]).astype(jnp.int32)
    block_start = jnp.arange(n_blocks, dtype=jnp.int32) * FFN_BLK
    block_expert = jnp.minimum(
        jnp.sum((pad_ends[None, :] <= block_start[:, None]).astype(jnp.int32), axis=1), N_EXP - 1)
    return dest, last_blk, block_expert


def kernel(x, w_in, w_ret_o, w_dsa_o, w_out, ln1_g, ln1_b, router_grp_w, router_grp_b,
           router_exp_w, router_exp_b, w_gate, w_up, w_down, ln2_g, ln2_b):
    B, S, D = x.shape
    T = B * S
    cdt = _MXU_DTYPE
    assert w_in.shape[-1] == _C_END
    t_ret = [jnp.asarray(t) for t in _rot_tables(S, RET_DK, 4 * RET_DK, "half")]
    t_att = [jnp.asarray(t) for t in _rot_tables(S, ATT_HD, 8 * ATT_HD, "half")]
    t_idx = [jnp.asarray(t) for t in _rot_tables(S, IDX_D, 8 * LANES, "spread")]
    ident = (np.ones((S, LANES), np.float32), np.zeros((S, LANES), np.float32))
    t_e = [jnp.asarray(np.concatenate([ta[:, :LANES], i, ti[:, :LANES], i], axis=1))
           for ta, ti, i in zip(_rot_tables(S, ATT_HD, LANES, "half"),
                                _rot_tables(S, IDX_D, LANES, "spread"), ident)]
    xf = x.reshape(T, D)
    xb = xf.astype(cdt)
    w_in_t = jnp.swapaxes(w_in, 1, 2)
    for l in range(DEPTH_):
        w_iq = _spread64(w_in_t[l, _C_IQ:_C_IK])
        w_e = jnp.concatenate([w_in_t[l, _C_AK:_C_IQ], _spread64(w_in_t[l, _C_IK:_C_IW]),
                               jnp.pad(w_in_t[l, _C_IW:_C_GR], ((0, LANES - IDX_HEADS), (0, 0)))],
                              axis=0)
        w_gates = w_in_t[l, _C_GR:_C_END]
        qk = _proj(xb, w_in_t, B, S, layer=l, col0=_C_RQ, n_cols=_C_RV - _C_RQ, rot=RET_DK,
                   tables=t_ret)
        vg = _proj(xb, w_in_t, B, S, layer=l, col0=_C_RV, n_cols=_C_AQ - _C_RV)
        aq = _proj(xb, w_in_t, B, S, layer=l, col0=_C_AQ, n_cols=_C_AK - _C_AQ, rot=ATT_HD,
                   tables=t_att)
        iq = _proj(xb, w_iq, B, S, rot=LANES, tables=t_idx)
        e = _proj(xb, w_e, B, S, rot=LANES, tables=t_e, out_dtype=jnp.float32)
        gates = _proj(xb, w_gates, B, S)
        ret = _retention(qk, vg, B, S)
        att = _dsa(aq, iq, e, B, S)
        rw = jnp.pad(jnp.concatenate([router_grp_w[l], router_exp_w[l]], axis=1),
                     ((0, 0), (0, LANES - N_GRP - N_EXP)))
        rb = jnp.pad(jnp.concatenate([router_grp_b[l], router_exp_b[l]]),
                     (0, LANES - N_GRP - N_EXP)).reshape(1, LANES)
        h, route, counts = _merge(ret, att, gates, xf, w_ret_o[l].astype(cdt),
                                  w_dsa_o[l].astype(cdt), w_out[l].astype(cdt),
                                  ln1_g[l].reshape(1, D), ln1_b[l].reshape(1, D), rw, rb)
        dest, last_blk, block_expert = _moe_plan(route, counts)
        x_pad = _dispatch(dest, last_blk, h, block_expert.shape[0])
        y_pad = _ffn(block_expert, x_pad, w_gate, w_up, w_down, l)
        xf, xb = _combine(dest, y_pad, route, h, ln2_g[l].reshape(1, D), ln2_b[l].reshape(1, D))
    return xf.reshape(B, S, D)
```

```python
import functools

import numpy as np
import jax
import jax.numpy as jnp
from jax import lax
from jax.experimental import pallas as pl
from jax.experimental.pallas import tpu as pltpu

D_MODEL_ = 1024
DEPTH_ = 2
RET_HEADS = 4
RET_DK = 256
RET_DV = 512
RET_C = 128
ATT_HEADS = 8
ATT_HD = 128
IDX_HEADS = 8
IDX_D = 64
TOPK = 256
QBLK = 128
N_GRP = 4
EXP_PER_GRP = 8
N_EXP = N_GRP * EXP_PER_GRP
EXP_FF = 512
FFN_BLK = 128
THETA = 10000.0
LN_EPS_ = 1e-5
GN_EPS_ = 1e-5
ALPHA = (2 * DEPTH_) ** 0.25

LANES = 128
SUBLANES = 8
INT_MIN = -(2 ** 31)
_CODE_NEG_INF = INT_MIN + 0x7FFFFF
_F32_LOWEST = float(np.finfo(np.float32).min)

_MXU_DTYPE = jnp.bfloat16
_VMEM_LIMIT = 56 * 1024 * 1024

ROW_TILES = D_MODEL_ // LANES
assert ROW_TILES == SUBLANES


def _cparams(sem):
    return pltpu.CompilerParams(dimension_semantics=sem, vmem_limit_bytes=_VMEM_LIMIT)


def _dot(a, b):
    return jnp.dot(a, b, preferred_element_type=jnp.float32)


def _dot_nt(a, b):
    return lax.dot_general(a, b, (((1,), (1,)), ((), ())), preferred_element_type=jnp.float32)


def _dot_tn(a, b):
    return lax.dot_general(a, b, (((0,), (0,)), ((), ())), preferred_element_type=jnp.float32)


def _tile_rows(ref, n, s):
    return ref.at[pl.ds(s, n, stride=ROW_TILES), :]


def _load_tiled(ref, n):
    return jnp.concatenate([_tile_rows(ref, n, s)[...] for s in range(ROW_TILES)], axis=1)


def _store_tiled(ref, val):
    n = val.shape[0]
    for s in range(ROW_TILES):
        _tile_rows(ref, n, s)[...] = val[:, s * LANES:(s + 1) * LANES]


def _swap_halves(a, d):
    tn = a.shape[1]
    if d == 2 * LANES:
        parts = []
        for c in range(0, tn, d):
            parts += [a[:, c + LANES:c + d], a[:, c:c + LANES]]
        return jnp.concatenate(parts, axis=1)
    assert d == LANES
    return jnp.concatenate(
        [pltpu.roll(a[:, c:c + LANES], LANES // 2, axis=1) for c in range(0, tn, LANES)], axis=1)


def _proj_kernel(x_ref, w_ref, *rest, rot):
    if rot:
        c_ref, s_ref, o_ref, w_s = rest
    else:
        o_ref, w_s = rest

    @pl.when((pl.program_id(1) == 0) & (pl.program_id(2) == 0))
    def _():
        w_s[...] = w_ref[...].astype(w_s.dtype)

    acc = _dot_nt(x_ref[...], w_s[...])
    if rot:
        acc = acc * c_ref[...] + _swap_halves(acc, rot) * s_ref[...]
    o_ref[...] = acc.astype(o_ref.dtype)


def _proj(x, wt, batch, seq, *, layer=None, col0=0, n_cols=None, rot=0, tables=None,
          out_dtype=None, tm=1024, tn=1024):
    T, K = x.shape
    n_cols = n_cols or wt.shape[-2]
    tn = min(tn, n_cols)
    tm = min(tm, seq)
    n_seq = seq // tm
    assert col0 % tn == 0 and n_cols % tn == 0 and seq % tm == 0
    c0 = col0 // tn
    row = lambda j, p, b: (b * n_seq + p, 0)
    if layer is None:
        w_spec = pl.BlockSpec((tn, K), lambda j, p, b: (c0 + j, 0))
    else:
        w_spec = pl.BlockSpec((None, tn, K), lambda j, p, b: (layer, c0 + j, 0))
    in_specs = [pl.BlockSpec((tm, K), row), w_spec]
    args = [x, wt]
    if rot:
        in_specs += [pl.BlockSpec((tm, tn), lambda j, p, b: (p, 0))] * 2
        args += list(tables)
    return pl.pallas_call(
        functools.partial(_proj_kernel, rot=rot),
        grid=(n_cols // tn, n_seq, batch),
        in_specs=in_specs,
        out_specs=pl.BlockSpec((tm, tn), lambda j, p, b: (b * n_seq + p, j)),
        out_shape=jax.ShapeDtypeStruct((T, n_cols), out_dtype or x.dtype),
        scratch_shapes=[pltpu.VMEM((tn, K), _MXU_DTYPE)],
        compiler_params=_cparams(("parallel", "arbitrary", "arbitrary")),
        name="proj_rot%d" % rot,
    )(*args)


def _rot_tables(seq, d, width, layout):
    inv = 1.0 / (THETA ** (np.arange(0, d, 2, dtype=np.float64) / d))
    ang = np.arange(seq, dtype=np.float64)[:, None] * inv[None, :]
    cos, sin = np.cos(ang), np.sin(ang)
    if layout == "half":
        c = np.concatenate([cos, cos], axis=1)
        s = np.concatenate([-sin, sin], axis=1)
    else:
        z = np.zeros_like(cos)
        c = np.concatenate([cos, z, cos, z], axis=1)
        s = np.concatenate([-sin, z, sin, z], axis=1)
    reps = width // c.shape[1]
    return np.tile(c, (1, reps)).astype(np.float32), np.tile(s, (1, reps)).astype(np.float32)


_RET_CHUNKS_PER_STEP = 4


def _ret_kernel(qk_ref, v_ref, g_ref, dec_ref, xi_ref, zeta_ref, o_ref, r_ref, *, gamma_c):
    H, DK, DV, C = RET_HEADS, RET_DK, RET_DV, RET_C

    @pl.when(pl.program_id(1) == 0)
    def _():
        r_ref[...] = jnp.zeros_like(r_ref)

    for h in range(H):
        xi = jnp.concatenate([xi_ref[h]] * (DV // LANES), axis=1)
        zeta = jnp.concatenate([zeta_ref[h]] * (DV // LANES), axis=1)
        for c in range(qk_ref.shape[0] // C):
            rows = slice(c * C, (c + 1) * C)
            q = qk_ref[rows, h * DK:(h + 1) * DK]
            k = qk_ref[rows, (H + h) * DK:(H + h + 1) * DK]
            v = v_ref[rows, h * DV:(h + 1) * DV]
            s = _dot_nt(q, k) * dec_ref[h]
            intra = _dot(s.astype(_MXU_DTYPE), v)
            r_old = r_ref[h]
            cross = _dot(q, r_old.astype(_MXU_DTYPE))
            y = intra + cross * xi
            vz = (v.astype(jnp.float32) * zeta).astype(_MXU_DTYPE)
            r_ref[h] = gamma_c[h] * r_old + _dot_tn(k, vz)
            mu = jnp.mean(y, axis=-1, keepdims=True)
            yc = y - mu
            var = jnp.mean(yc * yc, axis=-1, keepdims=True)
            yn = yc * lax.rsqrt(var + GN_EPS_)
            g = g_ref[rows, h * DV:(h + 1) * DV].astype(jnp.float32)
            o_ref[rows, h * DV:(h + 1) * DV] = (yn * (g * jax.nn.sigmoid(g))).astype(o_ref.dtype)


def _retention(qk, vg, batch, seq):
    H, DK, DV, C = RET_HEADS, RET_DK, RET_DV, RET_C
    rows = C * _RET_CHUNKS_PER_STEP
    n_steps = seq // rows
    lg = np.log(1.0 - 2.0 ** (-5.0 - np.arange(H, dtype=np.float64)))
    pos = np.arange(C, dtype=np.float64)
    diff = pos[:, None] - pos[None, :]
    dec = np.where(diff >= 0, np.exp(np.maximum(diff, 0.0)[None] * lg[:, None, None]), 0.0)
    dec = (dec * DK ** -0.5).astype(np.float32)
    ones = np.ones((1, 1, LANES))
    xi = (np.exp((pos[None, :] + 1.0) * lg[:, None])[:, :, None] * ones).astype(np.float32)
    zeta = (np.exp((C - 1.0 - pos)[None, :] * lg[:, None])[:, :, None] * ones
            * DK ** -0.5).astype(np.float32)
    gamma_c = tuple(float(g) for g in np.exp(C * lg))
    T = qk.shape[0]
    const3 = lambda b, n: (0, 0, 0)
    row = lambda b, n: (b * n_steps + n, 0)
    return pl.pallas_call(
        functools.partial(_ret_kernel, gamma_c=gamma_c),
        grid=(batch, n_steps),
        in_specs=[pl.BlockSpec((rows, 2 * H * DK), row),
                  pl.BlockSpec((rows, H * DV), row),
                  pl.BlockSpec((rows, H * DV), lambda b, n: (b * n_steps + n, 1)),
                  pl.BlockSpec((H, C, C), const3),
                  pl.BlockSpec((H, C, LANES), const3),
                  pl.BlockSpec((H, C, LANES), const3)],
        out_specs=pl.BlockSpec((rows, H * DV), row),
        out_shape=jax.ShapeDtypeStruct((T, H * DV), qk.dtype),
        scratch_shapes=[pltpu.VMEM((H, DK, DV), jnp.float32)],
        compiler_params=_cparams(("parallel", "arbitrary")),
        name="retention",
    )(qk, vg, vg, jnp.asarray(dec), jnp.asarray(xi), jnp.asarray(zeta))


_COUNT_CHAINS = 8
_DSA_CLASSES = 8


def _col_count(mask):
    n, qb = mask.shape
    part = jnp.sum(mask.astype(jnp.int32).reshape(_COUNT_CHAINS, n // _COUNT_CHAINS, qb), axis=1)
    return jnp.sum(part, axis=0, keepdims=True)


def _dsa_block(sv, j, aq_ref, iq_ref, eq_ref, o_ref, k_s, ik_s, vt_s, score_s, bias_s, topk):
    QB = aq_ref.shape[0]
    w_t = eq_ref[:, 3 * LANES:4 * LANES].T * ((IDX_D ** -0.5) * (IDX_HEADS ** -0.5))
    score = jnp.zeros((sv, QB), jnp.float32)
    for h in range(IDX_HEADS):
        rel = jnp.maximum(_dot_nt(ik_s[0:sv, :], iq_ref[:, h * LANES:(h + 1) * LANES]), 0.0)
        score = score + rel * w_t[h:h + 1, :]
    key_pos = lax.broadcasted_iota(jnp.int32, (sv, QB), 0)
    q_pos = j * QB + lax.broadcasted_iota(jnp.int32, (sv, QB), 1)
    score_s[0:sv, :] = jnp.where(key_pos <= q_pos, score, -jnp.inf)

    def count_ge(c):
        return _col_count(score_s[0:sv, :] >= c)

    def float_of(code):
        return pltpu.bitcast(jnp.where(code < 0, code ^ 0x7FFFFFFF, code), jnp.float32)

    code = jnp.where(count_ge(jnp.zeros((1, QB), jnp.float32)) >= topk, 0, INT_MIN)

    def bit_step(i, code):
        cand = code | lax.shift_left(jnp.int32(1), 30 - i)
        ok = (count_ge(float_of(cand)) >= topk) | (cand <= _CODE_NEG_INF)
        return jnp.where(ok, cand, code)

    code = lax.fori_loop(0, 31, bit_step, code)
    thr = float_of(code)
    thr_lo = jnp.maximum(thr, _F32_LOWEST)
    has_tie = jnp.max(jnp.where(code > _CODE_NEG_INF, count_ge(thr) - topk, 0)) > 0

    @pl.when(jnp.logical_not(has_tie))
    def _():
        bias_s[0:sv, :] = jnp.where(score_s[0:sv, :] >= thr_lo, 0.0, -jnp.inf)

    @pl.when(has_tie)
    def _():
        need_f = (topk - _col_count(score_s[0:sv, :] > thr)).astype(jnp.float32)
        r_i = lax.broadcasted_iota(jnp.int32, (LANES, LANES), 0)
        c_i = lax.broadcasted_iota(jnp.int32, (LANES, LANES), 1)
        lower = (r_i > c_i).astype(_MXU_DTYPE)
        carry = jnp.zeros((1, QB), jnp.float32)
        for c in range(sv // LANES):
            sl = slice(c * LANES, (c + 1) * LANES)
            sc = score_s[sl, :]
            eq_c = sc == thr
            eq_f = eq_c.astype(jnp.float32)
            before = _dot(lower, eq_f.astype(_MXU_DTYPE)) + carry
            take = ((sc > thr) | (eq_c & (before < need_f))) & (sc >= _F32_LOWEST)
            bias_s[sl, :] = jnp.where(take, 0.0, -jnp.inf)
            carry = carry + jnp.sum(eq_f, axis=0, keepdims=True)

    c_exp = (ATT_HD ** -0.5) * float(np.log2(np.e))
    for h in range(ATT_HEADS):
        s = _dot_nt(k_s[0:sv, :], aq_ref[:, h * ATT_HD:(h + 1) * ATT_HD]) + bias_s[0:sv, :]
        m = jnp.max(s, axis=0, keepdims=True)
        p = jnp.exp2((s - m) * c_exp)
        l = jnp.sum(p, axis=0, keepdims=True)
        o_t = _dot(vt_s[:, 0:sv], p.astype(_MXU_DTYPE)) / l
        o_ref[:, h * ATT_HD:(h + 1) * ATT_HD] = o_t.T.astype(o_ref.dtype)


def _dsa_kernel(aq_ref, iq_ref, eq_ref, ek_ref, o_ref, k_s, ik_s, vt_s, score_s, bias_s, *, topk):
    S = ek_ref.shape[0]
    QB = aq_ref.shape[0]
    j = pl.program_id(1)

    @pl.when(j == 0)
    def _():
        k_s[...] = ek_ref[:, 0:LANES].astype(k_s.dtype)
        vt_s[...] = ek_ref[:, LANES:2 * LANES].T.astype(vt_s.dtype)
        ik_s[...] = ek_ref[:, 2 * LANES:3 * LANES].astype(ik_s.dtype)

    n_blk = S // QB
    n_cls = min(_DSA_CLASSES, n_blk)
    per_class = n_blk // n_cls
    for c in range(n_cls):
        @pl.when((j >= c * per_class) & (j < (c + 1) * per_class))
        def _(c=c):
            _dsa_block((c + 1) * per_class * QB, j, aq_ref, iq_ref, eq_ref, o_ref,
                       k_s, ik_s, vt_s, score_s, bias_s, topk)


def _dsa(aq, iq, e, batch, seq):
    T = aq.shape[0]
    nb = seq // QBLK
    topk = min(TOPK, seq // 4)
    qrow = lambda b, j: (b * nb + j, 0)
    return pl.pallas_call(
        functools.partial(_dsa_kernel, topk=topk),
        grid=(batch, nb),
        in_specs=[pl.BlockSpec((QBLK, ATT_HEADS * ATT_HD), qrow),
                  pl.BlockSpec((QBLK, IDX_HEADS * LANES), qrow),
                  pl.BlockSpec((QBLK, 4 * LANES), qrow),
                  pl.BlockSpec((seq, 4 * LANES), lambda b, j: (b, 0))],
        out_specs=pl.BlockSpec((QBLK, ATT_HEADS * ATT_HD), qrow),
        out_shape=jax.ShapeDtypeStruct((T, ATT_HEADS * ATT_HD), aq.dtype),
        scratch_shapes=[pltpu.VMEM((seq, LANES), aq.dtype),
                        pltpu.VMEM((seq, LANES), aq.dtype),
                        pltpu.VMEM((LANES, seq), aq.dtype),
                        pltpu.VMEM((seq, QBLK), jnp.float32),
                        pltpu.VMEM((seq, QBLK), jnp.float32)],
        compiler_params=_cparams(("parallel", "arbitrary")),
        name="dsa",
    )(aq, iq, e, e)


def _layer_norm(z, g, b):
    mu = jnp.mean(z, axis=-1, keepdims=True)
    zc = z - mu
    var = jnp.mean(zc * zc, axis=-1, keepdims=True)
    return zc * lax.rsqrt(var + LN_EPS_) * g + b


def _split_hi_lo(a):
    hi = a.astype(_MXU_DTYPE)
    lo = (a - hi.astype(jnp.float32)).astype(_MXU_DTYPE)
    return hi, lo


def _merge_kernel(ret_ref, att_ref, gr_ref, ga_ref, x_ref, wr_ref, wd_ref, wo_ref, g_ref, b_ref,
                  rwh_ref, rwl_ref, rb_ref, h_ref, route_ref, cnt_ref, carry_s):
    @pl.when(pl.program_id(0) == 0)
    def _():
        carry_s[...] = jnp.zeros_like(carry_s)

    a = _dot(ret_ref[...], wr_ref[...])
    d = _dot(att_ref[...], wd_ref[...])
    merged = (jax.nn.sigmoid(gr_ref[...].astype(jnp.float32)) * a
              + jax.nn.sigmoid(ga_ref[...].astype(jnp.float32)) * d)
    mix = _dot(merged.astype(_MXU_DTYPE), wo_ref[...])
    h = _layer_norm(ALPHA * x_ref[...] + mix, g_ref[...], b_ref[...])
    _store_tiled(h_ref, h)
    h_hi, h_lo = _split_hi_lo(h)
    logits = (_dot(h_hi, rwh_ref[...]) + _dot(h_lo, rwh_ref[...]) + _dot(h_hi, rwl_ref[...])
              + rb_ref[...])
    route_ref[...] = _route_tile(logits, carry_s)
    cnt_ref[...] = carry_s[...]


def _merge(ret, att, gates, x, wr, wd, wo, g, b, rw, rb, tm=512):
    T, D = x.shape
    row = lambda i: (i, 0)
    const = lambda i: (0, 0)
    rw_hi, rw_lo = _split_hi_lo(rw)
    return pl.pallas_call(
        _merge_kernel,
        grid=(T // tm,),
        in_specs=[pl.BlockSpec((tm, ret.shape[1]), row),
                  pl.BlockSpec((tm, att.shape[1]), row),
                  pl.BlockSpec((tm, D), lambda i: (i, 0)),
                  pl.BlockSpec((tm, D), lambda i: (i, 1)),
                  pl.BlockSpec((tm, D), row),
                  pl.BlockSpec(wr.shape, const),
                  pl.BlockSpec(wd.shape, const),
                  pl.BlockSpec(wo.shape, const),
                  pl.BlockSpec((1, D), const),
                  pl.BlockSpec((1, D), const),
                  pl.BlockSpec(rw.shape, const),
                  pl.BlockSpec(rw.shape, const),
                  pl.BlockSpec((1, LANES), const)],
        out_specs=[pl.BlockSpec((tm * ROW_TILES, LANES), row), pl.BlockSpec((tm, LANES), row),
                   pl.BlockSpec((SUBLANES, LANES), const)],
        out_shape=[jax.ShapeDtypeStruct((T * ROW_TILES, LANES), jnp.float32),
                   jax.ShapeDtypeStruct((T, LANES), jnp.float32),
                   jax.ShapeDtypeStruct((SUBLANES, LANES), jnp.float32)],
        scratch_shapes=[pltpu.VMEM((SUBLANES, LANES), jnp.float32)],
        compiler_params=_cparams(("arbitrary",)),
        name="merge_ln_router",
    )(ret, att, gates, gates, x, wr, wd, wo, g, b, rw_hi, rw_lo, rb)


def _route_tile(lg, carry_s):
    tm = lg.shape[0]
    lane = lax.broadcasted_iota(jnp.int32, (tm, LANES), 1).astype(jnp.float32)
    neg = -jnp.inf

    def first_max(v):
        m = jnp.max(v, axis=-1, keepdims=True)
        idx = jnp.min(jnp.where(v == m, lane, float(LANES)), axis=-1, keepdims=True)
        return m, idx

    gl = jnp.where(lane < N_GRP, lg, neg)
    gmax, gsel = first_max(gl)
    p_grp = 1.0 / jnp.sum(jnp.exp(gl - gmax), axis=-1, keepdims=True)
    lo = N_GRP + gsel * EXP_PER_GRP
    el = jnp.where((lane >= lo) & (lane < lo + EXP_PER_GRP), lg, neg)
    top1, i1 = first_max(el)
    top2, i2 = first_max(jnp.where(lane == i1, neg, el))
    z = jnp.exp(top2 - top1)
    g1 = p_grp / (1.0 + z)
    g2 = p_grp * z / (1.0 + z)
    oh1 = lane == i1
    oh2 = lane == i2
    oh = (oh1 | oh2).astype(jnp.float32)
    r_i = lax.broadcasted_iota(jnp.int32, (tm, tm), 0)
    c_i = lax.broadcasted_iota(jnp.int32, (tm, tm), 1)
    lower = (r_i > c_i).astype(_MXU_DTYPE)
    before = _dot(lower, oh.astype(_MXU_DTYPE)) + carry_s[0:1, :]
    rank1 = jnp.sum(jnp.where(oh1, before, 0.0), axis=-1, keepdims=True)
    rank2 = jnp.sum(jnp.where(oh2, before, 0.0), axis=-1, keepdims=True)
    new_carry = carry_s[0:1, :] + jnp.sum(oh, axis=0, keepdims=True)
    carry_s[...] = jnp.broadcast_to(new_carry, carry_s.shape)
    e1 = i1 - N_GRP
    e2 = i2 - N_GRP
    out = jnp.zeros((tm, LANES), jnp.float32)
    for col, val in enumerate((e1, e2, rank1, rank2, g1, g2)):
        out = jnp.where(lane == col, val, out)
    return out


_GATHER_UNROLL = 8


def _token_copy(src_hbm, dst_vmem, sem, src_tok, dst_tok):
    src = src_hbm.at[pl.ds(pl.multiple_of(src_tok * ROW_TILES, ROW_TILES), ROW_TILES)]
    dst = dst_vmem.at[pl.ds(pl.multiple_of(dst_tok * ROW_TILES, ROW_TILES), ROW_TILES)]
    return pltpu.make_async_copy(src, dst, sem)


def _start_token_gather(idx_ref, src_hbm, dst_vmem, sem, n, stride=1, offset=0):
    def issue(r, c):
        _token_copy(src_hbm, dst_vmem, sem, idx_ref[0, 0, offset + r * stride], r).start()
        return c

    lax.fori_loop(0, n, issue, 0, unroll=_GATHER_UNROLL)


def _wait_token_gather(src_hbm, dst_vmem, sem, n):
    pltpu.make_async_copy(src_hbm.at[pl.ds(0, n * ROW_TILES)], dst_vmem, sem).wait()


def _dispatch_kernel(last_ref, dest_ref, h_ref, x_hbm, zero_s, sem):
    tm = h_ref.shape[0] // ROW_TILES
    blk_rows = zero_s.shape[0]
    n_blocks = x_hbm.shape[0] // blk_rows

    @pl.when(pl.program_id(0) == 0)
    def _():
        zero_s[...] = jnp.zeros_like(zero_s)

        def zero_block(blk):
            cp = pltpu.make_async_copy(
                zero_s, x_hbm.at[pl.ds(pl.multiple_of(blk * blk_rows, ROW_TILES), blk_rows)], sem)
            cp.start()
            cp.wait()

        def fill(e, c):
            blk = last_ref[e]

            @pl.when(blk >= 0)
            def _():
                zero_block(blk)
            return c

        lax.fori_loop(0, N_EXP, fill, 0)

        def fill_tail(blk, c):
            zero_block(blk)
            return c

        lax.fori_loop(last_ref[N_EXP], n_blocks, fill_tail, 0)

    def issue(r, c):
        src = h_ref.at[pl.ds(pl.multiple_of(r * ROW_TILES, ROW_TILES), ROW_TILES)]
        for j in range(2):
            d = dest_ref[0, 0, 2 * r + j]
            dst = x_hbm.at[pl.ds(pl.multiple_of(d * ROW_TILES, ROW_TILES), ROW_TILES)]
            pltpu.make_async_copy(src, dst, sem).start()
        return c

    lax.fori_loop(0, tm, issue, 0, unroll=_GATHER_UNROLL // 2)
    for _ in range(2):
        pltpu.make_async_copy(h_ref, x_hbm.at[pl.ds(0, tm * ROW_TILES)], sem).wait()


def _dispatch(dest, last_blk, h, n_blocks, tm=512):
    T = h.shape[0] // ROW_TILES
    n = T // tm
    blk_rows = FFN_BLK * ROW_TILES
    grid_spec = pltpu.PrefetchScalarGridSpec(
        num_scalar_prefetch=1,
        grid=(n,),
        in_specs=[pl.BlockSpec((1, 1, 2 * tm), lambda i, lb: (i, 0, 0), memory_space=pltpu.SMEM),
                  pl.BlockSpec((tm * ROW_TILES, LANES), lambda i, lb: (i, 0))],
        out_specs=pl.BlockSpec(memory_space=pl.ANY),
        scratch_shapes=[pltpu.VMEM((blk_rows, LANES), jnp.float32), pltpu.SemaphoreType.DMA(())],
    )
    return pl.pallas_call(
        _dispatch_kernel,
        grid_spec=grid_spec,
        out_shape=jax.ShapeDtypeStruct((n_blocks * blk_rows, LANES), jnp.float32),
        compiler_params=_cparams(("arbitrary",)),
        name="dispatch",
    )(last_blk, dest.reshape(n, 1, 2 * tm), h)


def _ffn_kernel(be_ref, x_ref, wg_ref, wu_ref, wd_ref, y_ref, wg_s, wu_s, wd_s):
    i = pl.program_id(0)

    @pl.when((i == 0) | (be_ref[i] != be_ref[jnp.maximum(i - 1, 0)]))
    def _():
        wg_s[...] = wg_ref[0].astype(wg_s.dtype)
        wu_s[...] = wu_ref[0].astype(wu_s.dtype)
        wd_s[...] = wd_ref[0].astype(wd_s.dtype)

    xb = _load_tiled(x_ref, FFN_BLK).astype(_MXU_DTYPE)
    gate = _dot(xb, wg_s[...])
    up = _dot(xb, wu_s[...])
    act = (gate * jax.nn.sigmoid(gate)) * up
    _store_tiled(y_ref, _dot(act.astype(_MXU_DTYPE), wd_s[...]))


def _ffn(block_expert, x_pad, wg, wu, wd, layer):
    n_blocks = block_expert.shape[0]
    D, F = wg.shape[2], wg.shape[3]
    wmap = lambda i, be: (layer, be[i], 0, 0)
    blk_rows = FFN_BLK * ROW_TILES
    blk = pl.BlockSpec((blk_rows, LANES), lambda i, be: (i, 0))
    grid_spec = pltpu.PrefetchScalarGridSpec(
        num_scalar_prefetch=1,
        grid=(n_blocks,),
        in_specs=[blk,
                  pl.BlockSpec((None, 1, D, F), wmap),
                  pl.BlockSpec((None, 1, D, F), wmap),
                  pl.BlockSpec((None, 1, F, D), wmap)],
        out_specs=blk,
        scratch_shapes=[pltpu.VMEM((D, F), _MXU_DTYPE),
                        pltpu.VMEM((D, F), _MXU_DTYPE),
                        pltpu.VMEM((F, D), _MXU_DTYPE)],
    )
    return pl.pallas_call(
        _ffn_kernel,
        grid_spec=grid_spec,
        out_shape=jax.ShapeDtypeStruct((n_blocks * blk_rows, LANES), jnp.float32),
        compiler_params=_cparams(("arbitrary",)),
        name="expert_ffn",
    )(block_expert, x_pad, wg, wu, wd)


def _combine_kernel(dest_ref, dest_next_ref, y_hbm, r_ref, h_ref, g_ref, b_ref, o_ref, ob_ref,
                    y_buf, sem):
    tm = o_ref.shape[0]
    i = pl.program_id(0)
    n = pl.num_programs(0)
    slot = i % 2

    def start(idx_ref, s):
        for c in range(2):
            _start_token_gather(idx_ref, y_hbm, y_buf.at[s, c], sem.at[s, c], tm, stride=2, offset=c)

    @pl.when(i == 0)
    def _():
        start(dest_ref, 0)

    @pl.when(i + 1 < n)
    def _():
        start(dest_next_ref, 1 - slot)

    for c in range(2):
        _wait_token_gather(y_hbm, y_buf.at[slot, c], sem.at[slot, c], tm)
    r = r_ref[...]
    g0 = r[:, 4:5]
    g1 = r[:, 5:6]
    z = jnp.concatenate(
        [ALPHA * _tile_rows(h_ref, tm, s)[...]
         + _tile_rows(y_buf.at[slot, 0], tm, s)[...] * g0
         + _tile_rows(y_buf.at[slot, 1], tm, s)[...] * g1 for s in range(ROW_TILES)], axis=1)
    out = _layer_norm(z, g_ref[...], b_ref[...])
    o_ref[...] = out
    ob_ref[...] = out.astype(ob_ref.dtype)


def _combine(dest, y_pad, route, h, g, b, tm=128):
    T = route.shape[0]
    D = g.shape[1]
    n = T // tm
    row = lambda i: (i, 0)
    const = lambda i: (0, 0)
    dest3 = dest.reshape(n, 1, 2 * tm)
    return pl.pallas_call(
        _combine_kernel,
        grid=(n,),
        in_specs=[pl.BlockSpec((1, 1, 2 * tm), lambda i: (i, 0, 0), memory_space=pltpu.SMEM),
                  pl.BlockSpec((1, 1, 2 * tm), lambda i: (jnp.minimum(i + 1, n - 1), 0, 0),
                               memory_space=pltpu.SMEM),
                  pl.BlockSpec(memory_space=pl.ANY),
                  pl.BlockSpec((tm, LANES), row),
                  pl.BlockSpec((tm * ROW_TILES, LANES), row),
                  pl.BlockSpec((1, D), const),
                  pl.BlockSpec((1, D), const)],
        out_specs=[pl.BlockSpec((tm, D), row), pl.BlockSpec((tm, D), row)],
        out_shape=[jax.ShapeDtypeStruct((T, D), jnp.float32),
                   jax.ShapeDtypeStruct((T, D), _MXU_DTYPE)],
        scratch_shapes=[pltpu.VMEM((2, 2, tm * ROW_TILES, LANES), jnp.float32),
                        pltpu.SemaphoreType.DMA((2, 2))],
        compiler_params=_cparams(("arbitrary",)),
        name="combine_ln",
    )(dest3, dest3, y_pad, route, h, g, b)


def _spread64(wt):
    n, K = wt.shape[0] // IDX_D, wt.shape[1]
    wt = wt.reshape(n, 2, IDX_D // 2, K)
    wt = jnp.pad(wt, ((0, 0), (0, 0), (0, IDX_D // 2), (0, 0)))
    return wt.reshape(n * LANES, K)


_C_RQ = 0
_C_RV = 2 * RET_HEADS * RET_DK
_C_AQ = _C_RV + 2 * RET_HEADS * RET_DV
_C_AK = _C_AQ + ATT_HEADS * ATT_HD
_C_IQ = _C_AK + 2 * ATT_HD
_C_IK = _C_IQ + IDX_HEADS * IDX_D
_C_IW = _C_IK + IDX_D
_C_GR = _C_IW + IDX_HEADS
_C_END = _C_GR + 2 * D_MODEL_


def _moe_plan(route, counts):
    T = route.shape[0]
    n_assign = T * 2
    n_blocks = -(-n_assign // FFN_BLK) + N_EXP
    cnt = counts[0, N_GRP:N_GRP + N_EXP].astype(jnp.int32)
    padded = (cnt + FFN_BLK - 1) // FFN_BLK * FFN_BLK
    pad_ends = jnp.cumsum(padded)
    pad_starts = pad_ends - padded
    expert = route[:, 0:2].astype(jnp.int32)
    rank = route[:, 2:4].astype(jnp.int32)
    is_e = expert[:, :, None] == jnp.arange(N_EXP, dtype=jnp.int32)
    dest = (jnp.sum(jnp.where(is_e, pad_starts, 0), axis=-1) + rank).reshape(n_assign)
    last_blk = jnp.concatenate([jnp.where(padded > 0, pad_ends // FFN_BLK - 1, -1),
                                pad_ends[-1:] // FFN_BLK]).astype(jnp.int32)
    block_start = jnp.arange(n_blocks, dtype=jnp.int32) * FFN_BLK
    block_expert = jnp.minimum(
        jnp.sum((pad_ends[None, :] <= block_start[:, None]).astype(jnp.int32), axis=1), N_EXP - 1)
    return dest, last_blk, block_expert


def kernel(x, w_in, w_ret_o, w_dsa_o, w_out, ln1_g, ln1_b, router_grp_w, router_grp_b,
           router_exp_w, router_exp_b, w_gate, w_up, w_down, ln2_g, ln2_b):
    B, S, D = x.shape
    T = B * S
    cdt = _MXU_DTYPE
    assert w_in.shape[-1] == _C_END
    t_ret = [jnp.asarray(t) for t in _rot_tables(S, RET_DK, 4 * RET_DK, "half")]
    t_att = [jnp.asarray(t) for t in _rot_tables(S, ATT_HD, 8 * ATT_HD, "half")]
    t_idx = [jnp.asarray(t) for t in _rot_tables(S, IDX_D, 8 * LANES, "spread")]
    ident = (np.ones((S, LANES), np.float32), np.zeros((S, LANES), np.float32))
    t_e = [jnp.asarray(np.concatenate([ta[:, :LANES], i, ti[:, :LANES], i], axis=1))
           for ta, ti, i in zip(_rot_tables(S, ATT_HD, LANES, "half"),
                                _rot_tables(S, IDX_D, LANES, "spread"), ident)]
    xf = x.reshape(T, D)
    xb = xf.astype(cdt)
    w_in_t = jnp.swapaxes(w_in, 1, 2)
    for l in range(DEPTH_):
        w_iq = _spread64(w_in_t[l, _C_IQ:_C_IK])
        w_e = jnp.concatenate([w_in_t[l, _C_AK:_C_IQ], _spread64(w_in_t[l, _C_IK:_C_IW]),
                               jnp.pad(w_in_t[l, _C_IW:_C_GR], ((0, LANES - IDX_HEADS), (0, 0)))],
                              axis=0)
        w_gates = w_in_t[l, _C_GR:_C_END]
        qk = _proj(xb, w_in_t, B, S, layer=l, col0=_C_RQ, n_cols=_C_RV - _C_RQ, rot=RET_DK,
                   tables=t_ret)
        vg = _proj(xb, w_in_t, B, S, layer=l, col0=_C_RV, n_cols=_C_AQ - _C_RV)
        aq = _proj(xb, w_in_t, B, S, layer=l, col0=_C_AQ, n_cols=_C_AK - _C_AQ, rot=ATT_HD,
                   tables=t_att)
        iq = _proj(xb, w_iq, B, S, rot=LANES, tables=t_idx)
        e = _proj(xb, w_e, B, S, rot=LANES, tables=t_e, out_dtype=jnp.float32)
        gates = _proj(xb, w_gates, B, S)
        ret = _retention(qk, vg, B, S)
        att = _dsa(aq, iq, e, B, S)
        rw = jnp.pad(jnp.concatenate([router_grp_w[l], router_exp_w[l]], axis=1),
                     ((0, 0), (0, LANES - N_GRP - N_EXP)))
        rb = jnp.pad(jnp.concatenate([router_grp_b[l], router_exp_b[l]]),
                     (0, LANES - N_GRP - N_EXP)).reshape(1, LANES)
        h, route, counts = _merge(ret, att, gates, xf, w_ret_o[l].astype(cdt),
                                  w_dsa_o[l].astype(cdt), w_out[l].astype(cdt),
                                  ln1_g[l].reshape(1, D), ln1_b[l].reshape(1, D), rw, rb)
        dest, last_blk, block_expert = _moe_plan(route, counts)
        x_pad = _dispatch(dest, last_blk, h, block_expert.shape[0])
        y_pad = _ffn(block_expert, x_pad, w_gate, w_up, w_down, l)
        xf, xb = _combine(dest, y_pad, route, h, ln2_g[l].reshape(1, D), ln2_b[l].reshape(1, D))
    return xf.reshape(B, S, D)
```

```python
import functools

import numpy as np
import jax
import jax.numpy as jnp
from jax import lax
from jax.experimental import pallas as pl
from jax.experimental.pallas import tpu as pltpu

D_MODEL_ = 1024
DEPTH_ = 2
RET_HEADS = 4
RET_DK = 256
RET_DV = 512
RET_C = 128
ATT_HEADS = 8
ATT_HD = 128
IDX_HEADS = 8
IDX_D = 64
TOPK = 256
QBLK = 128
N_GRP = 4
EXP_PER_GRP = 8
N_EXP = N_GRP * EXP_PER_GRP
EXP_FF = 512
FFN_BLK = 256
THETA = 10000.0
LN_EPS_ = 1e-5
GN_EPS_ = 1e-5
ALPHA = (2 * DEPTH_) ** 0.25

LANES = 128
SUBLANES = 8
INT_MIN = -(2 ** 31)
_CODE_NEG_INF = INT_MIN + 0x7FFFFF
_F32_LOWEST = float(np.finfo(np.float32).min)

_MXU_DTYPE = jnp.bfloat16
_VMEM_LIMIT = 56 * 1024 * 1024

ROW_TILES = D_MODEL_ // LANES
assert ROW_TILES == SUBLANES


def _cparams(sem):
    return pltpu.CompilerParams(dimension_semantics=sem, vmem_limit_bytes=_VMEM_LIMIT)


def _dot(a, b):
    return jnp.dot(a, b, preferred_element_type=jnp.float32)


def _dot_nt(a, b):
    return lax.dot_general(a, b, (((1,), (1,)), ((), ())), preferred_element_type=jnp.float32)


def _dot_tn(a, b):
    return lax.dot_general(a, b, (((0,), (0,)), ((), ())), preferred_element_type=jnp.float32)


def _tile_rows(ref, n, s):
    return ref.at[pl.ds(s, n, stride=ROW_TILES), :]


def _load_tiled(ref, n):
    return jnp.concatenate([_tile_rows(ref, n, s)[...] for s in range(ROW_TILES)], axis=1)


def _store_tiled(ref, val):
    n = val.shape[0]
    for s in range(ROW_TILES):
        _tile_rows(ref, n, s)[...] = val[:, s * LANES:(s + 1) * LANES]


def _swap_halves(a, d):
    tn = a.shape[1]
    if d == 2 * LANES:
        parts = []
        for c in range(0, tn, d):
            parts += [a[:, c + LANES:c + d], a[:, c:c + LANES]]
        return jnp.concatenate(parts, axis=1)
    assert d == LANES
    return jnp.concatenate(
        [pltpu.roll(a[:, c:c + LANES], LANES // 2, axis=1) for c in range(0, tn, LANES)], axis=1)


def _proj_kernel(x_ref, w_ref, *rest, rot):
    if rot:
        c_ref, s_ref, o_ref, w_s = rest
    else:
        o_ref, w_s = rest

    @pl.when((pl.program_id(1) == 0) & (pl.program_id(2) == 0))
    def _():
        w_s[...] = w_ref[...].astype(w_s.dtype)

    acc = _dot_nt(x_ref[...], w_s[...])
    if rot:
        acc = acc * c_ref[...] + _swap_halves(acc, rot) * s_ref[...]
    o_ref[...] = acc.astype(o_ref.dtype)


def _proj(x, wt, batch, seq, *, layer=None, col0=0, n_cols=None, rot=0, tables=None,
          out_dtype=None, tm=1024, tn=1024):
    T, K = x.shape
    n_cols = n_cols or wt.shape[-2]
    tn = min(tn, n_cols)
    tm = min(tm, seq)
    n_seq = seq // tm
    assert col0 % tn == 0 and n_cols % tn == 0 and seq % tm == 0
    c0 = col0 // tn
    row = lambda j, p, b: (b * n_seq + p, 0)
    if layer is None:
        w_spec = pl.BlockSpec((tn, K), lambda j, p, b: (c0 + j, 0))
    else:
        w_spec = pl.BlockSpec((None, tn, K), lambda j, p, b: (layer, c0 + j, 0))
    in_specs = [pl.BlockSpec((tm, K), row), w_spec]
    args = [x, wt]
    if rot:
        in_specs += [pl.BlockSpec((tm, tn), lambda j, p, b: (p, 0))] * 2
        args += list(tables)
    return pl.pallas_call(
        functools.partial(_proj_kernel, rot=rot),
        grid=(n_cols // tn, n_seq, batch),
        in_specs=in_specs,
        out_specs=pl.BlockSpec((tm, tn), lambda j, p, b: (b * n_seq + p, j)),
        out_shape=jax.ShapeDtypeStruct((T, n_cols), out_dtype or x.dtype),
        scratch_shapes=[pltpu.VMEM((tn, K), _MXU_DTYPE)],
        compiler_params=_cparams(("parallel", "arbitrary", "arbitrary")),
        name="proj_rot%d" % rot,
    )(*args)


def _rot_tables(seq, d, width, layout):
    inv = 1.0 / (THETA ** (np.arange(0, d, 2, dtype=np.float64) / d))
    ang = np.arange(seq, dtype=np.float64)[:, None] * inv[None, :]
    cos, sin = np.cos(ang), np.sin(ang)
    if layout == "half":
        c = np.concatenate([cos, cos], axis=1)
        s = np.concatenate([-sin, sin], axis=1)
    else:
        z = np.zeros_like(cos)
        c = np.concatenate([cos, z, cos, z], axis=1)
        s = np.concatenate([-sin, z, sin, z], axis=1)
    reps = width // c.shape[1]
    return np.tile(c, (1, reps)).astype(np.float32), np.tile(s, (1, reps)).astype(np.float32)


_RET_CHUNKS_PER_STEP = 4


def _ret_kernel(qk_ref, v_ref, g_ref, dec_ref, xi_ref, zeta_ref, o_ref, r_ref, *, gamma_c):
    H, DK, DV, C = RET_HEADS, RET_DK, RET_DV, RET_C

    @pl.when(pl.program_id(1) == 0)
    def _():
        r_ref[...] = jnp.zeros_like(r_ref)

    for h in range(H):
        xi = jnp.concatenate([xi_ref[h]] * (DV // LANES), axis=1)
        zeta = jnp.concatenate([zeta_ref[h]] * (DV // LANES), axis=1)
        for c in range(qk_ref.shape[0] // C):
            rows = slice(c * C, (c + 1) * C)
            q = qk_ref[rows, h * DK:(h + 1) * DK]
            k = qk_ref[rows, (H + h) * DK:(H + h + 1) * DK]
            v = v_ref[rows, h * DV:(h + 1) * DV]
            s = _dot_nt(q, k) * dec_ref[h]
            intra = _dot(s.astype(_MXU_DTYPE), v)
            r_old = r_ref[h]
            cross = _dot(q, r_old.astype(_MXU_DTYPE))
            y = intra + cross * xi
            vz = (v.astype(jnp.float32) * zeta).astype(_MXU_DTYPE)
            r_ref[h] = gamma_c[h] * r_old + _dot_tn(k, vz)
            mu = jnp.mean(y, axis=-1, keepdims=True)
            yc = y - mu
            var = jnp.mean(yc * yc, axis=-1, keepdims=True)
            yn = yc * lax.rsqrt(var + GN_EPS_)
            g = g_ref[rows, h * DV:(h + 1) * DV].astype(jnp.float32)
            o_ref[rows, h * DV:(h + 1) * DV] = (yn * (g * jax.nn.sigmoid(g))).astype(o_ref.dtype)


def _retention(qk, vg, batch, seq):
    H, DK, DV, C = RET_HEADS, RET_DK, RET_DV, RET_C
    rows = C * _RET_CHUNKS_PER_STEP
    n_steps = seq // rows
    lg = np.log(1.0 - 2.0 ** (-5.0 - np.arange(H, dtype=np.float64)))
    pos = np.arange(C, dtype=np.float64)
    diff = pos[:, None] - pos[None, :]
    dec = np.where(diff >= 0, np.exp(np.maximum(diff, 0.0)[None] * lg[:, None, None]), 0.0)
    dec = (dec * DK ** -0.5).astype(np.float32)
    ones = np.ones((1, 1, LANES))
    xi = (np.exp((pos[None, :] + 1.0) * lg[:, None])[:, :, None] * ones).astype(np.float32)
    zeta = (np.exp((C - 1.0 - pos)[None, :] * lg[:, None])[:, :, None] * ones
            * DK ** -0.5).astype(np.float32)
    gamma_c = tuple(float(g) for g in np.exp(C * lg))
    T = qk.shape[0]
    const3 = lambda b, n: (0, 0, 0)
    row = lambda b, n: (b * n_steps + n, 0)
    return pl.pallas_call(
        functools.partial(_ret_kernel, gamma_c=gamma_c),
        grid=(batch, n_steps),
        in_specs=[pl.BlockSpec((rows, 2 * H * DK), row),
                  pl.BlockSpec((rows, H * DV), row),
                  pl.BlockSpec((rows, H * DV), lambda b, n: (b * n_steps + n, 1)),
                  pl.BlockSpec((H, C, C), const3),
                  pl.BlockSpec((H, C, LANES), const3),
                  pl.BlockSpec((H, C, LANES), const3)],
        out_specs=pl.BlockSpec((rows, H * DV), row),
        out_shape=jax.ShapeDtypeStruct((T, H * DV), qk.dtype),
        scratch_shapes=[pltpu.VMEM((H, DK, DV), jnp.float32)],
        compiler_params=_cparams(("parallel", "arbitrary")),
        name="retention",
    )(qk, vg, vg, jnp.asarray(dec), jnp.asarray(xi), jnp.asarray(zeta))


_COUNT_CHAINS = 8
_DSA_CLASSES = 8


def _col_count(mask):
    n, qb = mask.shape
    part = jnp.sum(mask.astype(jnp.int32).reshape(_COUNT_CHAINS, n // _COUNT_CHAINS, qb), axis=1)
    return jnp.sum(part, axis=0, keepdims=True)


def _dsa_block(sv, j, aq_ref, iq_ref, eq_ref, o_ref, k_s, ik_s, vt_s, score_s, bias_s, topk):
    QB = aq_ref.shape[0]
    w_t = eq_ref[:, 3 * LANES:4 * LANES].T * ((IDX_D ** -0.5) * (IDX_HEADS ** -0.5))
    score = jnp.zeros((sv, QB), jnp.float32)
    for h in range(IDX_HEADS):
        rel = jnp.maximum(_dot_nt(ik_s[0:sv, :], iq_ref[:, h * LANES:(h + 1) * LANES]), 0.0)
        score = score + rel * w_t[h:h + 1, :]
    key_pos = lax.broadcasted_iota(jnp.int32, (sv, QB), 0)
    q_pos = j * QB + lax.broadcasted_iota(jnp.int32, (sv, QB), 1)
    score_s[0:sv, :] = jnp.where(key_pos <= q_pos, score, -jnp.inf)

    def count_ge(c):
        return _col_count(score_s[0:sv, :] >= c)

    def float_of(code):
        return pltpu.bitcast(jnp.where(code < 0, code ^ 0x7FFFFFFF, code), jnp.float32)

    code = jnp.where(count_ge(jnp.zeros((1, QB), jnp.float32)) >= topk, 0, INT_MIN)

    def bit_step(i, code):
        cand = code | lax.shift_left(jnp.int32(1), 30 - i)
        ok = (count_ge(float_of(cand)) >= topk) | (cand <= _CODE_NEG_INF)
        return jnp.where(ok, cand, code)

    code = lax.fori_loop(0, 31, bit_step, code)
    thr = float_of(code)
    thr_lo = jnp.maximum(thr, _F32_LOWEST)
    has_tie = jnp.max(jnp.where(code > _CODE_NEG_INF, count_ge(thr) - topk, 0)) > 0

    @pl.when(jnp.logical_not(has_tie))
    def _():
        bias_s[0:sv, :] = jnp.where(score_s[0:sv, :] >= thr_lo, 0.0, -jnp.inf)

    @pl.when(has_tie)
    def _():
        need_f = (topk - _col_count(score_s[0:sv, :] > thr)).astype(jnp.float32)
        r_i = lax.broadcasted_iota(jnp.int32, (LANES, LANES), 0)
        c_i = lax.broadcasted_iota(jnp.int32, (LANES, LANES), 1)
        lower = (r_i > c_i).astype(_MXU_DTYPE)
        carry = jnp.zeros((1, QB), jnp.float32)
        for c in range(sv // LANES):
            sl = slice(c * LANES, (c + 1) * LANES)
            sc = score_s[sl, :]
            eq_c = sc == thr
            eq_f = eq_c.astype(jnp.float32)
            before = _dot(lower, eq_f.astype(_MXU_DTYPE)) + carry
            take = ((sc > thr) | (eq_c & (before < need_f))) & (sc >= _F32_LOWEST)
            bias_s[sl, :] = jnp.where(take, 0.0, -jnp.inf)
            carry = carry + jnp.sum(eq_f, axis=0, keepdims=True)

    c_exp = (ATT_HD ** -0.5) * float(np.log2(np.e))
    for h in range(ATT_HEADS):
        s = _dot_nt(k_s[0:sv, :], aq_ref[:, h * ATT_HD:(h + 1) * ATT_HD]) + bias_s[0:sv, :]
        m = jnp.max(s, axis=0, keepdims=True)
        p = jnp.exp2((s - m) * c_exp)
        l = jnp.sum(p, axis=0, keepdims=True)
        o_t = _dot(vt_s[:, 0:sv], p.astype(_MXU_DTYPE)) / l
        o_ref[:, h * ATT_HD:(h + 1) * ATT_HD] = o_t.T.astype(o_ref.dtype)


def _dsa_kernel(aq_ref, iq_ref, eq_ref, ek_ref, o_ref, k_s, ik_s, vt_s, score_s, bias_s, *, topk):
    S = ek_ref.shape[0]
    QB = aq_ref.shape[0]
    j = pl.program_id(1)

    @pl.when(j == 0)
    def _():
        k_s[...] = ek_ref[:, 0:LANES].astype(k_s.dtype)
        vt_s[...] = ek_ref[:, LANES:2 * LANES].T.astype(vt_s.dtype)
        ik_s[...] = ek_ref[:, 2 * LANES:3 * LANES].astype(ik_s.dtype)

    n_blk = S // QB
    n_cls = min(_DSA_CLASSES, n_blk)
    per_class = n_blk // n_cls
    for c in range(n_cls):
        @pl.when((j >= c * per_class) & (j < (c + 1) * per_class))
        def _(c=c):
            _dsa_block((c + 1) * per_class * QB, j, aq_ref, iq_ref, eq_ref, o_ref,
                       k_s, ik_s, vt_s, score_s, bias_s, topk)


def _dsa(aq, iq, e, batch, seq):
    T = aq.shape[0]
    nb = seq // QBLK
    topk = min(TOPK, seq // 4)
    qrow = lambda b, j: (b * nb + j, 0)
    return pl.pallas_call(
        functools.partial(_dsa_kernel, topk=topk),
        grid=(batch, nb),
        in_specs=[pl.BlockSpec((QBLK, ATT_HEADS * ATT_HD), qrow),
                  pl.BlockSpec((QBLK, IDX_HEADS * LANES), qrow),
                  pl.BlockSpec((QBLK, 4 * LANES), qrow),
                  pl.BlockSpec((seq, 4 * LANES), lambda b, j: (b, 0))],
        out_specs=pl.BlockSpec((QBLK, ATT_HEADS * ATT_HD), qrow),
        out_shape=jax.ShapeDtypeStruct((T, ATT_HEADS * ATT_HD), aq.dtype),
        scratch_shapes=[pltpu.VMEM((seq, LANES), aq.dtype),
                        pltpu.VMEM((seq, LANES), aq.dtype),
                        pltpu.VMEM((LANES, seq), aq.dtype),
                        pltpu.VMEM((seq, QBLK), jnp.float32),
                        pltpu.VMEM((seq, QBLK), jnp.float32)],
        compiler_params=_cparams(("parallel", "arbitrary")),
        name="dsa",
    )(aq, iq, e, e)


def _layer_norm(z, g, b):
    mu = jnp.mean(z, axis=-1, keepdims=True)
    zc = z - mu
    var = jnp.mean(zc * zc, axis=-1, keepdims=True)
    return zc * lax.rsqrt(var + LN_EPS_) * g + b


def _split_hi_lo(a):
    hi = a.astype(_MXU_DTYPE)
    lo = (a - hi.astype(jnp.float32)).astype(_MXU_DTYPE)
    return hi, lo


def _merge_kernel(ret_ref, att_ref, gr_ref, ga_ref, x_ref, wr_ref, wd_ref, wo_ref, g_ref, b_ref,
                  rwh_ref, rwl_ref, rb_ref, h_ref, route_ref, cnt_ref, carry_s):
    @pl.when(pl.program_id(0) == 0)
    def _():
        carry_s[...] = jnp.zeros_like(carry_s)

    a = _dot(ret_ref[...], wr_ref[...])
    d = _dot(att_ref[...], wd_ref[...])
    merged = (jax.nn.sigmoid(gr_ref[...].astype(jnp.float32)) * a
              + jax.nn.sigmoid(ga_ref[...].astype(jnp.float32)) * d)
    mix = _dot(merged.astype(_MXU_DTYPE), wo_ref[...])
    h = _layer_norm(ALPHA * x_ref[...] + mix, g_ref[...], b_ref[...])
    _store_tiled(h_ref, h)
    h_hi, h_lo = _split_hi_lo(h)
    logits = (_dot(h_hi, rwh_ref[...]) + _dot(h_lo, rwh_ref[...]) + _dot(h_hi, rwl_ref[...])
              + rb_ref[...])
    route_ref[...] = _route_tile(logits, carry_s)
    cnt_ref[...] = carry_s[...]


def _merge(ret, att, gates, x, wr, wd, wo, g, b, rw, rb, tm=512):
    T, D = x.shape
    row = lambda i: (i, 0)
    const = lambda i: (0, 0)
    rw_hi, rw_lo = _split_hi_lo(rw)
    return pl.pallas_call(
        _merge_kernel,
        grid=(T // tm,),
        in_specs=[pl.BlockSpec((tm, ret.shape[1]), row),
                  pl.BlockSpec((tm, att.shape[1]), row),
                  pl.BlockSpec((tm, D), lambda i: (i, 0)),
                  pl.BlockSpec((tm, D), lambda i: (i, 1)),
                  pl.BlockSpec((tm, D), row),
                  pl.BlockSpec(wr.shape, const),
                  pl.BlockSpec(wd.shape, const),
                  pl.BlockSpec(wo.shape, const),
                  pl.BlockSpec((1, D), const),
                  pl.BlockSpec((1, D), const),
                  pl.BlockSpec(rw.shape, const),
                  pl.BlockSpec(rw.shape, const),
                  pl.BlockSpec((1, LANES), const)],
        out_specs=[pl.BlockSpec((tm * ROW_TILES, LANES), row), pl.BlockSpec((tm, LANES), row),
                   pl.BlockSpec((SUBLANES, LANES), const)],
        out_shape=[jax.ShapeDtypeStruct((T * ROW_TILES, LANES), jnp.float32),
                   jax.ShapeDtypeStruct((T, LANES), jnp.float32),
                   jax.ShapeDtypeStruct((SUBLANES, LANES), jnp.float32)],
        scratch_shapes=[pltpu.VMEM((SUBLANES, LANES), jnp.float32)],
        compiler_params=_cparams(("arbitrary",)),
        name="merge_ln_router",
    )(ret, att, gates, gates, x, wr, wd, wo, g, b, rw_hi, rw_lo, rb)


def _route_tile(lg, carry_s):
    tm = lg.shape[0]
    lane = lax.broadcasted_iota(jnp.int32, (tm, LANES), 1).astype(jnp.float32)
    neg = -jnp.inf

    def first_max(v):
        m = jnp.max(v, axis=-1, keepdims=True)
        idx = jnp.min(jnp.where(v == m, lane, float(LANES)), axis=-1, keepdims=True)
        return m, idx

    gl = jnp.where(lane < N_GRP, lg, neg)
    gmax, gsel = first_max(gl)
    p_grp = 1.0 / jnp.sum(jnp.exp(gl - gmax), axis=-1, keepdims=True)
    lo = N_GRP + gsel * EXP_PER_GRP
    el = jnp.where((lane >= lo) & (lane < lo + EXP_PER_GRP), lg, neg)
    top1, i1 = first_max(el)
    top2, i2 = first_max(jnp.where(lane == i1, neg, el))
    z = jnp.exp(top2 - top1)
    g1 = p_grp / (1.0 + z)
    g2 = p_grp * z / (1.0 + z)
    oh1 = lane == i1
    oh2 = lane == i2
    oh = (oh1 | oh2).astype(jnp.float32)
    r_i = lax.broadcasted_iota(jnp.int32, (tm, tm), 0)
    c_i = lax.broadcasted_iota(jnp.int32, (tm, tm), 1)
    lower = (r_i > c_i).astype(_MXU_DTYPE)
    before = _dot(lower, oh.astype(_MXU_DTYPE)) + carry_s[0:1, :]
    rank1 = jnp.sum(jnp.where(oh1, before, 0.0), axis=-1, keepdims=True)
    rank2 = jnp.sum(jnp.where(oh2, before, 0.0), axis=-1, keepdims=True)
    new_carry = carry_s[0:1, :] + jnp.sum(oh, axis=0, keepdims=True)
    carry_s[...] = jnp.broadcast_to(new_carry, carry_s.shape)
    e1 = i1 - N_GRP
    e2 = i2 - N_GRP
    out = jnp.zeros((tm, LANES), jnp.float32)
    for col, val in enumerate((e1, e2, rank1, rank2, g1, g2)):
        out = jnp.where(lane == col, val, out)
    return out


_GATHER_UNROLL = 8


def _token_copy(src_hbm, dst_vmem, sem, src_tok, dst_tok):
    src = src_hbm.at[pl.ds(pl.multiple_of(src_tok * ROW_TILES, ROW_TILES), ROW_TILES)]
    dst = dst_vmem.at[pl.ds(pl.multiple_of(dst_tok * ROW_TILES, ROW_TILES), ROW_TILES)]
    return pltpu.make_async_copy(src, dst, sem)


def _start_token_gather(idx_ref, src_hbm, dst_vmem, sem, n, stride=1, offset=0):
    def issue(r, c):
        _token_copy(src_hbm, dst_vmem, sem, idx_ref[0, 0, offset + r * stride], r).start()
        return c

    lax.fori_loop(0, n, issue, 0, unroll=_GATHER_UNROLL)


def _wait_token_gather(src_hbm, dst_vmem, sem, n):
    pltpu.make_async_copy(src_hbm.at[pl.ds(0, n * ROW_TILES)], dst_vmem, sem).wait()


def _dispatch_kernel(last_ref, dest_ref, h_ref, x_hbm, zero_s, sem):
    tm = h_ref.shape[0] // ROW_TILES
    blk_rows = zero_s.shape[0]
    n_blocks = x_hbm.shape[0] // blk_rows

    @pl.when(pl.program_id(0) == 0)
    def _():
        zero_s[...] = jnp.zeros_like(zero_s)

        def zero_block(blk):
            cp = pltpu.make_async_copy(
                zero_s, x_hbm.at[pl.ds(pl.multiple_of(blk * blk_rows, ROW_TILES), blk_rows)], sem)
            cp.start()
            cp.wait()

        def fill(e, c):
            blk = last_ref[e]

            @pl.when(blk >= 0)
            def _():
                zero_block(blk)
            return c

        lax.fori_loop(0, N_EXP, fill, 0)

        def fill_tail(blk, c):
            zero_block(blk)
            return c

        lax.fori_loop(last_ref[N_EXP], n_blocks, fill_tail, 0)

    def issue(r, c):
        src = h_ref.at[pl.ds(pl.multiple_of(r * ROW_TILES, ROW_TILES), ROW_TILES)]
        for j in range(2):
            d = dest_ref[0, 0, 2 * r + j]
            dst = x_hbm.at[pl.ds(pl.multiple_of(d * ROW_TILES, ROW_TILES), ROW_TILES)]
            pltpu.make_async_copy(src, dst, sem).start()
        return c

    lax.fori_loop(0, tm, issue, 0, unroll=_GATHER_UNROLL // 2)
    for _ in range(2):
        pltpu.make_async_copy(h_ref, x_hbm.at[pl.ds(0, tm * ROW_TILES)], sem).wait()


def _dispatch(dest, last_blk, h, n_blocks, tm=512):
    T = h.shape[0] // ROW_TILES
    n = T // tm
    blk_rows = FFN_BLK * ROW_TILES
    grid_spec = pltpu.PrefetchScalarGridSpec(
        num_scalar_prefetch=1,
        grid=(n,),
        in_specs=[pl.BlockSpec((1, 1, 2 * tm), lambda i, lb: (i, 0, 0), memory_space=pltpu.SMEM),
                  pl.BlockSpec((tm * ROW_TILES, LANES), lambda i, lb: (i, 0))],
        out_specs=pl.BlockSpec(memory_space=pl.ANY),
        scratch_shapes=[pltpu.VMEM((blk_rows, LANES), jnp.float32), pltpu.SemaphoreType.DMA(())],
    )
    return pl.pallas_call(
        _dispatch_kernel,
        grid_spec=grid_spec,
        out_shape=jax.ShapeDtypeStruct((n_blocks * blk_rows, LANES), jnp.float32),
        compiler_params=_cparams(("arbitrary",)),
        name="dispatch",
    )(last_blk, dest.reshape(n, 1, 2 * tm), h)


def _ffn_kernel(be_ref, x_ref, wg_ref, wu_ref, wd_ref, y_ref, wg_s, wu_s, wd_s):
    i = pl.program_id(0)

    @pl.when((i == 0) | (be_ref[i] != be_ref[jnp.maximum(i - 1, 0)]))
    def _():
        wg_s[...] = wg_ref[0].astype(wg_s.dtype)
        wu_s[...] = wu_ref[0].astype(wu_s.dtype)
        wd_s[...] = wd_ref[0].astype(wd_s.dtype)

    xb = _load_tiled(x_ref, FFN_BLK).astype(_MXU_DTYPE)
    gate = _dot(xb, wg_s[...])
    up = _dot(xb, wu_s[...])
    act = (gate * jax.nn.sigmoid(gate)) * up
    _store_tiled(y_ref, _dot(act.astype(_MXU_DTYPE), wd_s[...]))


def _ffn(block_expert, x_pad, wg, wu, wd, layer):
    n_blocks = block_expert.shape[0]
    D, F = wg.shape[2], wg.shape[3]
    wmap = lambda i, be: (layer, be[i], 0, 0)
    blk_rows = FFN_BLK * ROW_TILES
    blk = pl.BlockSpec((blk_rows, LANES), lambda i, be: (i, 0))
    grid_spec = pltpu.PrefetchScalarGridSpec(
        num_scalar_prefetch=1,
        grid=(n_blocks,),
        in_specs=[blk,
                  pl.BlockSpec((None, 1, D, F), wmap),
                  pl.BlockSpec((None, 1, D, F), wmap),
                  pl.BlockSpec((None, 1, F, D), wmap)],
        out_specs=blk,
        scratch_shapes=[pltpu.VMEM((D, F), _MXU_DTYPE),
                        pltpu.VMEM((D, F), _MXU_DTYPE),
                        pltpu.VMEM((F, D), _MXU_DTYPE)],
    )
    return pl.pallas_call(
        _ffn_kernel,
        grid_spec=grid_spec,
        out_shape=jax.ShapeDtypeStruct((n_blocks * blk_rows, LANES), jnp.float32),
        compiler_params=_cparams(("arbitrary",)),
        name="expert_ffn",
    )(block_expert, x_pad, wg, wu, wd)


def _combine_kernel(dest_ref, dest_next_ref, y_hbm, r_ref, h_ref, g_ref, b_ref, o_ref, ob_ref,
                    y_buf, sem):
    tm = o_ref.shape[0]
    i = pl.program_id(0)
    n = pl.num_programs(0)
    slot = i % 2

    def start(idx_ref, s):
        for c in range(2):
            _start_token_gather(idx_ref, y_hbm, y_buf.at[s, c], sem.at[s, c], tm, stride=2, offset=c)

    @pl.when(i == 0)
    def _():
        start(dest_ref, 0)

    @pl.when(i + 1 < n)
    def _():
        start(dest_next_ref, 1 - slot)

    for c in range(2):
        _wait_token_gather(y_hbm, y_buf.at[slot, c], sem.at[slot, c], tm)
    r = r_ref[...]
    g0 = r[:, 4:5]
    g1 = r[:, 5:6]
    z = jnp.concatenate(
        [ALPHA * _tile_rows(h_ref, tm, s)[...]
         + _tile_rows(y_buf.at[slot, 0], tm, s)[...] * g0
         + _tile_rows(y_buf.at[slot, 1], tm, s)[...] * g1 for s in range(ROW_TILES)], axis=1)
    out = _layer_norm(z, g_ref[...], b_ref[...])
    o_ref[...] = out
    ob_ref[...] = out.astype(ob_ref.dtype)


def _combine(dest, y_pad, route, h, g, b, tm=128):
    T = route.shape[0]
    D = g.shape[1]
    n = T // tm
    row = lambda i: (i, 0)
    const = lambda i: (0, 0)
    dest3 = dest.reshape(n, 1, 2 * tm)
    return pl.pallas_call(
        _combine_kernel,
        grid=(n,),
        in_specs=[pl.BlockSpec((1, 1, 2 * tm), lambda i: (i, 0, 0), memory_space=pltpu.SMEM),
                  pl.BlockSpec((1, 1, 2 * tm), lambda i: (jnp.minimum(i + 1, n - 1), 0, 0),
                               memory_space=pltpu.SMEM),
                  pl.BlockSpec(memory_space=pl.ANY),
                  pl.BlockSpec((tm, LANES), row),
                  pl.BlockSpec((tm * ROW_TILES, LANES), row),
                  pl.BlockSpec((1, D), const),
                  pl.BlockSpec((1, D), const)],
        out_specs=[pl.BlockSpec((tm, D), row), pl.BlockSpec((tm, D), row)],
        out_shape=[jax.ShapeDtypeStruct((T, D), jnp.float32),
                   jax.ShapeDtypeStruct((T, D), _MXU_DTYPE)],
        scratch_shapes=[pltpu.VMEM((2, 2, tm * ROW_TILES, LANES), jnp.float32),
                        pltpu.SemaphoreType.DMA((2, 2))],
        compiler_params=_cparams(("arbitrary",)),
        name="combine_ln",
    )(dest3, dest3, y_pad, route, h, g, b)


def _spread64(wt):
    n, K = wt.shape[0] // IDX_D, wt.shape[1]
    wt = wt.reshape(n, 2, IDX_D // 2, K)
    wt = jnp.pad(wt, ((0, 0), (0, 0), (0, IDX_D // 2), (0, 0)))
    return wt.reshape(n * LANES, K)


_C_RQ = 0
_C_RV = 2 * RET_HEADS * RET_DK
_C_AQ = _C_RV + 2 * RET_HEADS * RET_DV
_C_AK = _C_AQ + ATT_HEADS * ATT_HD
_C_IQ = _C_AK + 2 * ATT_HD
_C_IK = _C_IQ + IDX_HEADS * IDX_D
_C_IW = _C_IK + IDX_D
_C_GR = _C_IW + IDX_HEADS
_C_END = _C_GR + 2 * D_MODEL_


def _moe_plan(route, counts):
    T = route.shape[0]
    n_assign = T * 2
    n_blocks = -(-n_assign // FFN_BLK) + N_EXP
    cnt = counts[0, N_GRP:N_GRP + N_EXP].astype(jnp.int32)
    padded = (cnt + FFN_BLK - 1) // FFN_BLK * FFN_BLK
    pad_ends = jnp.cumsum(padded)
    pad_starts = pad_ends - padded
    expert = route[:, 0:2].astype(jnp.int32)
    rank = route[:, 2:4].astype(jnp.int32)
    is_e = expert[:, :, None] == jnp.arange(N_EXP, dtype=jnp.int32)
    dest = (jnp.sum(jnp.where(is_e, pad_starts, 0), axis=-1) + rank).reshape(n_assign)
    last_blk = jnp.concatenate([jnp.where(padded > 0, pad_ends // FFN_BLK - 1, -1),
                                pad_ends[-1:] // FFN_BLK]).astype(jnp.int32)
    block_start = jnp.arange(n_blocks, dtype=jnp.int32) * FFN_BLK
    block_expert = jnp.minimum(
        jnp.sum((pad_ends[None, :] <= block_start[:, None]).astype(jnp.int32), axis=1), N_EXP - 1)
    return dest, last_blk, block_expert


def kernel(x, w_in, w_ret_o, w_dsa_o, w_out, ln1_g, ln1_b, router_grp_w, router_grp_b,
           router_exp_w, router_exp_b, w_gate, w_up, w_down, ln2_g, ln2_b):
    B, S, D = x.shape
    T = B * S
    cdt = _MXU_DTYPE
    assert w_in.shape[-1] == _C_END
    t_ret = [jnp.asarray(t) for t in _rot_tables(S, RET_DK, 4 * RET_DK, "half")]
    t_att = [jnp.asarray(t) for t in _rot_tables(S, ATT_HD, 8 * ATT_HD, "half")]
    t_idx = [jnp.asarray(t) for t in _rot_tables(S, IDX_D, 8 * LANES, "spread")]
    ident = (np.ones((S, LANES), np.float32), np.zeros((S, LANES), np.float32))
    t_e = [jnp.asarray(np.concatenate([ta[:, :LANES], i, ti[:, :LANES], i], axis=1))
           for ta, ti, i in zip(_rot_tables(S, ATT_HD, LANES, "half"),
                                _rot_tables(S, IDX_D, LANES, "spread"), ident)]
    xf = x.reshape(T, D)
    xb = xf.astype(cdt)
    w_in_t = jnp.swapaxes(w_in, 1, 2)
    for l in range(DEPTH_):
        w_iq = _spread64(w_in_t[l, _C_IQ:_C_IK])
        w_e = jnp.concatenate([w_in_t[l, _C_AK:_C_IQ], _spread64(w_in_t[l, _C_IK:_C_IW]),
                               jnp.pad(w_in_t[l, _C_IW:_C_GR], ((0, LANES - IDX_HEADS), (0, 0)))],
                              axis=0)
        w_gates = w_in_t[l, _C_GR:_C_END]
        qk = _proj(xb, w_in_t, B, S, layer=l, col0=_C_RQ, n_cols=_C_RV - _C_RQ, rot=RET_DK,
                   tables=t_ret)
        vg = _proj(xb, w_in_t, B, S, layer=l, col0=_C_RV, n_cols=_C_AQ - _C_RV)
        aq = _proj(xb, w_in_t, B, S, layer=l, col0=_C_AQ, n_cols=_C_AK - _C_AQ, rot=ATT_HD,
                   tables=t_att)
        iq = _proj(xb, w_iq, B, S, rot=LANES, tables=t_idx)
        e = _proj(xb, w_e, B, S, rot=LANES, tables=t_e, out_dtype=jnp.float32)
        gates = _proj(xb, w_gates, B, S)
        ret = _retention(qk, vg, B, S)
        att = _dsa(aq, iq, e, B, S)
        rw = jnp.pad(jnp.concatenate([router_grp_w[l], router_exp_w[l]], axis=1),
                     ((0, 0), (0, LANES - N_GRP - N_EXP)))
        rb = jnp.pad(jnp.concatenate([router_grp_b[l], router_exp_b[l]]),
                     (0, LANES - N_GRP - N_EXP)).reshape(1, LANES)
        h, route, counts = _merge(ret, att, gates, xf, w_ret_o[l].astype(cdt),
                                  w_dsa_o[l].astype(cdt), w_out[l].astype(cdt),
                                  ln1_g[l].reshape(1, D), ln1_b[l].reshape(1, D), rw, rb)
        dest, last_blk, block_expert = _moe_plan(route, counts)
        x_pad = _dispatch(dest, last_blk, h, block_expert.shape[0])
        y_pad = _ffn(block_expert, x_pad, w_gate, w_up, w_down, l)
        xf, xb = _combine(dest, y_pad, route, h, ln2_g[l].reshape(1, D), ln2_b[l].reshape(1, D))
    return xf.reshape(B, S, D)
```
